```python
import math
import jax
import jax.numpy as jnp
from jax import lax
import numpy as np

D_MODEL = 1024
BATCH = 4
SEQ = 8192
DEPTH = 1
DEC_BATCH = 8
DEC_SEQ = 32
PAST_LEN = 2048

CHUNK = 64
HEAD_DIM = 64
N_HEADS = D_MODEL // (2 * HEAD_DIM)
D_ATTN = N_HEADS * 2 * HEAD_DIM
D_POOL = D_MODEL
POOL_WINDOWS = (2, 4, 8, 16)
N_POOL_GROUPS = len(POOL_WINDOWS)
POOL_GROUP = D_POOL // N_POOL_GROUPS
POOL_HIST = max(POOL_WINDOWS) - 1
N_GATES = 2
D_IN = 3 * D_ATTN + D_POOL + N_GATES * D_MODEL
D_FF = ((8 * D_MODEL) // 3 + 255) // 256 * 256
NUM_BUCKETS = 32
MAX_DISTANCE = 128
Q_BLOCK = 128
EPS = 1e-6
SUBLN_EPS = 1e-5

kernel_name = "hybrid_diffattn_multiscale_pool_stream_step"


def _rmsnorm(x, g, eps=EPS):
    xf = x.astype(jnp.float32)
    y = xf * lax.rsqrt(jnp.mean(xf * xf, axis=-1, keepdims=True) + eps)
    return (y * g.astype(jnp.float32)).astype(x.dtype)


def _rel_bucket(rel):
    nb = NUM_BUCKETS // 2
    ret = jnp.where(rel > 0, nb, 0)
    n = jnp.abs(rel)
    max_exact = nb // 2
    large = max_exact + (jnp.log(jnp.maximum(n, 1).astype(jnp.float32) / max_exact)
                         / math.log(MAX_DISTANCE / max_exact) * (nb - max_exact)).astype(jnp.int32)
    large = jnp.minimum(large, nb - 1)
    return ret + jnp.where(n < max_exact, n, large)


def _lambda(layer_idx, lq1, lk1, lq2, lk2):
    lam_init = 0.8 - 0.6 * math.exp(-0.3 * layer_idx)
    f32 = jnp.float32
    lam = (jnp.exp(jnp.sum(lq1.astype(f32) * lk1.astype(f32)))
           - jnp.exp(jnp.sum(lq2.astype(f32) * lk2.astype(f32))) + lam_init)
    return lam, lam_init


def _diff_attn(q, k, v, q_pos, k_pos, rel_bias, lam):
    q1, q2 = q[..., :HEAD_DIM], q[..., HEAD_DIM:]
    k1, k2 = k[..., :HEAD_DIM], k[..., HEAD_DIM:]
    scale = HEAD_DIM ** -0.5
    bias = jnp.transpose(rel_bias[_rel_bucket(k_pos[None, :] - q_pos[:, None])],
                         (2, 0, 1)).astype(jnp.float32)
    mask = (k_pos[None, :] // CHUNK) <= (q_pos[:, None] // CHUNK)

    def probs(qa, ka):
        s = jnp.einsum('bqhd,bkhd->bhqk', qa, ka).astype(jnp.float32) * scale + bias
        s = jnp.where(mask, s, -jnp.inf)
        return jax.nn.softmax(s, axis=-1)

    a = probs(q1, k1) - lam * probs(q2, k2)
    return jnp.einsum('bhqk,bkhe->bqhe', a.astype(v.dtype), v)


def _prompt_attn(q, k, v, rel_bias, lam):
    b, t = q.shape[0], q.shape[1]
    k_pos = jnp.arange(t)

    def block(i):
        start = i * Q_BLOCK
        qb = lax.dynamic_slice_in_dim(q, start, Q_BLOCK, axis=1)
        q_pos = start + jnp.arange(Q_BLOCK)
        return _diff_attn(qb, k, v, q_pos, k_pos, rel_bias, lam)

    o = lax.map(block, jnp.arange(t // Q_BLOCK))
    return jnp.moveaxis(o, 0, 1).reshape(b, t, N_HEADS, 2 * HEAD_DIM)


def _attn_out(o, subln_g, lam_init):
    b, t = o.shape[0], o.shape[1]
    o = _rmsnorm(o, subln_g, SUBLN_EPS) * (1.0 - lam_init)
    return o.reshape(b, t, D_ATTN)


def _project(x, norm_mix, w_in, b_gate):
    b, t = x.shape[0], x.shape[1]
    h = _rmsnorm(x, norm_mix)
    z = h @ w_in
    hd = (b, t, N_HEADS, 2 * HEAD_DIM)
    q = z[..., :D_ATTN].reshape(hd)
    k = z[..., D_ATTN:2 * D_ATTN].reshape(hd)
    v = z[..., 2 * D_ATTN:3 * D_ATTN].reshape(hd)
    u = z[..., 3 * D_ATTN:3 * D_ATTN + D_POOL]
    g = jax.nn.sigmoid(z[..., 3 * D_ATTN + D_POOL:] + b_gate)
    return q, k, v, u, g[..., :D_MODEL], g[..., D_MODEL:]


def _pool_branch(u, hist, t0, w_pool, pool_scale):
    b, t, c = u.shape
    xcat = jnp.concatenate([hist, u], axis=1)
    cs0 = jnp.pad(jnp.cumsum(xcat.astype(jnp.float32), axis=1), ((0, 0), (1, 0), (0, 0)))
    uf = u.astype(jnp.float32)
    pos = t0 + jnp.arange(t)
    hi = POOL_HIST + 1
    outs = []
    for gi, w in enumerate(POOL_WINDOWS):
        sl = slice(gi * POOL_GROUP, (gi + 1) * POOL_GROUP)
        s = cs0[:, hi:hi + t, sl] - cs0[:, hi - w:hi - w + t, sl]
        cnt = jnp.minimum(pos + 1, w).astype(jnp.float32)[None, :, None]
        outs.append(s / cnt - uf[..., sl])
    d = jnp.stack(outs, axis=2).astype(u.dtype)
    y = jnp.einsum('btgc,gcd->btgd', d, w_pool).reshape(b, t, c)
    return y * pool_scale, xcat[:, -POOL_HIST:]


def _merge_ffn(x, attn, pool, g_a, g_p, w_out, norm_ffn, w_gate_up, w_down):
    x = x + (g_a * attn + g_p * pool) @ w_out
    h = _rmsnorm(x, norm_ffn)
    gu = h @ w_gate_up
    return x + (jax.nn.silu(gu[..., :D_FF]) * gu[..., D_FF:]) @ w_down


def setup_inputs(seed: int = 0) -> dict:
    key = jax.random.key(seed)
    ks = jax.random.split(key, 24)
    f32 = jnp.float32
    nrm = lambda k, s, sc: jax.random.normal(k, s, f32) * sc
    return {
        'x_prompt': nrm(ks[0], (BATCH, SEQ, D_MODEL), 1.0),
        'x_sample': nrm(ks[1], (DEC_BATCH, DEC_SEQ, D_MODEL), 1.0),
        'cache_k': nrm(ks[2], (DEPTH, DEC_BATCH, PAST_LEN, N_HEADS, 2 * HEAD_DIM), 1.0),
        'cache_v': nrm(ks[3], (DEPTH, DEC_BATCH, PAST_LEN, N_HEADS, 2 * HEAD_DIM), 1.0),
        'state_pool': nrm(ks[4], (DEPTH, DEC_BATCH, POOL_HIST, D_POOL), 1.0),
        'rel_bias': nrm(ks[5], (NUM_BUCKETS, N_HEADS), 0.5),
        'norm_mix': 1.0 + nrm(ks[6], (DEPTH, D_MODEL), 0.02),
        'w_in': nrm(ks[7], (DEPTH, D_MODEL, D_IN), D_MODEL ** -0.5),
        'b_gate': nrm(ks[8], (DEPTH, N_GATES * D_MODEL), 0.02),
        'lambda_q1': nrm(ks[9], (DEPTH, HEAD_DIM), 0.1),
        'lambda_k1': nrm(ks[10], (DEPTH, HEAD_DIM), 0.1),
        'lambda_q2': nrm(ks[11], (DEPTH, HEAD_DIM), 0.1),
        'lambda_k2': nrm(ks[12], (DEPTH, HEAD_DIM), 0.1),
        'subln_g': 1.0 + nrm(ks[13], (DEPTH, 2 * HEAD_DIM), 0.02),
        'w_pool': nrm(ks[14], (DEPTH, N_POOL_GROUPS, POOL_GROUP, POOL_GROUP), POOL_GROUP ** -0.5),
        'pool_scale': 1.0 + nrm(ks[15], (DEPTH, D_POOL), 0.1),
        'w_out': nrm(ks[16], (DEPTH, D_MODEL, D_MODEL), D_MODEL ** -0.5),
        'norm_ffn': 1.0 + nrm(ks[17], (DEPTH, D_MODEL), 0.02),
        'w_gate_up': nrm(ks[18], (DEPTH, D_MODEL, 2 * D_FF), D_MODEL ** -0.5),
        'w_down': nrm(ks[19], (DEPTH, D_FF, D_MODEL), D_FF ** -0.5),
        'norm_final': 1.0 + nrm(ks[20], (D_MODEL,), 0.02),
    }


def reference(x_prompt, x_sample, cache_k, cache_v, state_pool, rel_bias, norm_mix, w_in,
              b_gate, lambda_q1, lambda_k1, lambda_q2, lambda_k2, subln_g, w_pool,
              pool_scale, w_out, norm_ffn, w_gate_up, w_down, norm_final):
    xp, xs = x_prompt, x_sample
    past = cache_k.shape[2]
    s_new = x_sample.shape[1]
    kp_l, vp_l, pp_l, ks_l, vs_l, ps_l = [], [], [], [], [], []
    for l in range(DEPTH):
        lam, lam_init = _lambda(l, lambda_q1[l], lambda_k1[l], lambda_q2[l], lambda_k2[l])
        q, k, v, u, g_a, g_p = _project(xp, norm_mix[l], w_in[l], b_gate[l])
        attn = _attn_out(_prompt_attn(q, k, v, rel_bias, lam), subln_g[l], lam_init)
        zeros_hist = jnp.zeros((xp.shape[0], POOL_HIST, D_POOL), u.dtype)
        pool, pool_st = _pool_branch(u, zeros_hist, 0, w_pool[l], pool_scale[l])
        xp = _merge_ffn(xp, attn, pool, g_a, g_p, w_out[l], norm_ffn[l], w_gate_up[l], w_down[l])
        kp_l.append(k)
        vp_l.append(v)
        pp_l.append(pool_st)
        q, k, v, u, g_a, g_p = _project(xs, norm_mix[l], w_in[l], b_gate[l])
        k_all = jnp.concatenate([cache_k[l], k], axis=1)
        v_all = jnp.concatenate([cache_v[l], v], axis=1)
        q_pos = past + jnp.arange(s_new)
        k_pos = jnp.arange(past + s_new)
        attn = _attn_out(_diff_attn(q, k_all, v_all, q_pos, k_pos, rel_bias, lam),
                         subln_g[l], lam_init)
        pool, pool_st = _pool_branch(u, state_pool[l], past, w_pool[l], pool_scale[l])
        xs = _merge_ffn(xs, attn, pool, g_a, g_p, w_out[l], norm_ffn[l], w_gate_up[l], w_down[l])
        ks_l.append(k)
        vs_l.append(v)
        ps_l.append(pool_st)
    y_prompt = _rmsnorm(xp, norm_final)
    y_sample = _rmsnorm(xs, norm_final)
    k_prompt = jnp.stack(kp_l)
    v_prompt = jnp.stack(vp_l)
    pool_prompt = jnp.stack(pp_l)
    k_sample = jnp.stack(ks_l)
    v_sample = jnp.stack(vs_l)
    pool_sample = jnp.stack(ps_l)
    return (y_prompt, y_sample, k_prompt, v_prompt, pool_prompt, k_sample, v_sample, pool_sample)
```

```python
import functools
import math

import numpy as np
import jax
import jax.numpy as jnp
from jax import lax
from jax.experimental import pallas as pl
from jax.experimental.pallas import tpu as pltpu

F32 = jnp.float32
BF16 = jnp.bfloat16

D_MODEL = 1024
CHUNK = 64
HEAD_DIM = 64
DV = 2 * HEAD_DIM
N_HEADS = D_MODEL // DV
POOL_WINDOWS = (2, 4, 8, 16)
POOL_GROUP = D_MODEL // len(POOL_WINDOWS)
POOL_HIST = max(POOL_WINDOWS) - 1
HALO = POOL_HIST + 1
D_FF = ((8 * D_MODEL) // 3 + 255) // 256 * 256
NUM_BUCKETS = 32
MAX_DISTANCE = 128
EPS = 1e-6
SUBLN_EPS = 1e-5
LAM_INIT = 0.8 - 0.6 * math.exp(-0.3 * 0)
QK_SCALE = HEAD_DIM ** -0.5
MASKED = -1e30

ATTN_BLOCK = 256
PROJ_ROWS = 512
FFN_ROWS = 256
VMEM_LIMIT = 56 * 1024 * 1024


def _rmsnorm(x, g, eps):
    return x * lax.rsqrt(jnp.mean(x * x, axis=-1, keepdims=True) + eps) * g


def _const_spec(shape):
    return pl.BlockSpec(shape, lambda *_: (0,) * len(shape), pipeline_mode=pl.Buffered(1))


def _proj_kernel(x_ref, g_ref, w_ref, *out_refs, rows, blocked):
    x = x_ref[0]
    h = _rmsnorm(x, g_ref[...], EPS).astype(BF16)
    q = jnp.dot(h, w_ref[:, 0:D_MODEL], preferred_element_type=F32) * QK_SCALE
    k = jnp.dot(h, w_ref[:, D_MODEL:2 * D_MODEL], preferred_element_type=F32)
    v = jnp.dot(h, w_ref[:, 2 * D_MODEL:3 * D_MODEL], preferred_element_type=F32)
    if not blocked:
        q_ref, k_ref, v_ref = out_refs
        q_ref[0] = q
        k_ref[0] = k
        v_ref[0] = v
        return
    k_ref, v_ref, kb_ref, qt_ref, vt_ref = out_refs
    k_ref[0] = k
    v_ref[0] = v
    qt = q.T
    vt = v.T
    for hd in range(N_HEADS):
        hs = slice(hd * DV, (hd + 1) * DV)
        for c in range(rows // ATTN_BLOCK):
            cs = slice(c * ATTN_BLOCK, (c + 1) * ATTN_BLOCK)
            kb_ref[0, hd, c] = k[cs, hs].astype(BF16)
            qt_ref[0, hd, c] = qt[hs, cs].astype(BF16)
            vt_ref[0, hd, c] = vt[hs, cs].astype(BF16)


def _project(x, norm_mix, w_qkv, *, rows, blocked):
    b, t, d = x.shape
    nb_step = rows // ATTN_BLOCK
    nat = jax.ShapeDtypeStruct((b, t, d), F32)
    nat_spec = pl.BlockSpec((1, rows, d), lambda i, j: (i, j, 0))
    if blocked:
        nb = t // ATTN_BLOCK
        out_shape = [nat, nat,
                     jax.ShapeDtypeStruct((b, N_HEADS, nb, ATTN_BLOCK, DV), BF16),
                     jax.ShapeDtypeStruct((b, N_HEADS, nb, DV, ATTN_BLOCK), BF16),
                     jax.ShapeDtypeStruct((b, N_HEADS, nb, DV, ATTN_BLOCK), BF16)]
        out_specs = [nat_spec, nat_spec,
                     pl.BlockSpec((1, N_HEADS, nb_step, ATTN_BLOCK, DV), lambda i, j: (i, 0, j, 0, 0)),
                     pl.BlockSpec((1, N_HEADS, nb_step, DV, ATTN_BLOCK), lambda i, j: (i, 0, j, 0, 0)),
                     pl.BlockSpec((1, N_HEADS, nb_step, DV, ATTN_BLOCK), lambda i, j: (i, 0, j, 0, 0))]
    else:
        out_shape = [nat, nat, nat]
        out_specs = [nat_spec, nat_spec, nat_spec]
    return pl.pallas_call(
        functools.partial(_proj_kernel, rows=rows, blocked=blocked),
        grid=(b, t // rows),
        in_specs=[nat_spec, _const_spec((1, d)), _const_spec((d, 3 * d))],
        out_specs=out_specs,
        out_shape=out_shape,
        compiler_params=pltpu.CompilerParams(
            dimension_semantics=("arbitrary", "arbitrary"), vmem_limit_bytes=VMEM_LIMIT),
        name="qkv_proj_blocked" if blocked else "qkv_proj",
    )(x, norm_mix, w_qkv)


def _rel_bucket_np(rel):
    nb = NUM_BUCKETS // 2
    ret = np.where(rel > 0, nb, 0)
    n = np.abs(rel)
    max_exact = nb // 2
    large = max_exact + (np.log(np.maximum(n, 1).astype(np.float64) / max_exact)
                         / math.log(MAX_DISTANCE / max_exact) * (nb - max_exact)).astype(np.int64)
    large = np.minimum(large, nb - 1)
    return (ret + np.where(n < max_exact, n, large)).astype(np.int32)


FAR_BUCKET = int(_rel_bucket_np(np.array([-2 * ATTN_BLOCK]))[0])


def _lambda_from(lv):
    s1 = jnp.sum(lv[0:1] * lv[1:2], axis=-1, keepdims=True)
    s2 = jnp.sum(lv[2:3] * lv[3:4], axis=-1, keepdims=True)
    return jnp.exp(s1) - jnp.exp(s2) + LAM_INIT


def _flash_kernel(qt_ref, kb_ref, vt_ref, bias_ref, lv_ref, g_ref, o_ref,
                  ml_ref, acc1_ref, acc2_ref):
    i = pl.program_id(2)
    blk = ATTN_BLOCK
    qt = qt_ref[0, 0, 0]
    zeros = jnp.zeros((HEAD_DIM, blk), BF16)
    w1 = jnp.concatenate([qt[:HEAD_DIM], zeros], axis=0)
    w2 = jnp.concatenate([zeros, qt[HEAD_DIM:]], axis=0)

    ml_ref[0:1] = jnp.full((1, blk), MASKED, F32)
    ml_ref[1:2] = jnp.zeros((1, blk), F32)
    ml_ref[2:3] = jnp.full((1, blk), MASKED, F32)
    ml_ref[3:4] = jnp.zeros((1, blk), F32)
    acc1_ref[...] = jnp.zeros_like(acc1_ref)
    acc2_ref[...] = jnp.zeros_like(acc2_ref)

    def one_map(kj, vj, w, bias, row, acc_ref):
        s = jnp.dot(kj, w, preferred_element_type=F32)
        if bias is not None:
            s = s + bias
        m_old = ml_ref[row:row + 1]
        l_old = ml_ref[row + 1:row + 2]
        m_new = jnp.maximum(m_old, jnp.max(s, axis=0, keepdims=True))
        alpha = jnp.exp(m_old - m_new)
        p = jnp.exp(s - m_new)
        ml_ref[row:row + 1] = m_new
        ml_ref[row + 1:row + 2] = alpha * l_old + jnp.sum(p, axis=0, keepdims=True)
        pv = jnp.dot(vj, p.astype(BF16), preferred_element_type=F32)
        acc_ref[...] = alpha * acc_ref[...] + pv

    def step(j, bias):
        kj = kb_ref[0, 0, j]
        vj = vt_ref[0, 0, j]
        one_map(kj, vj, w1, bias, 0, acc1_ref)
        one_map(kj, vj, w2, bias, 2, acc2_ref)

    @pl.loop(0, jnp.maximum(i - 1, 0))
    def _(j):
        step(j, None)

    @pl.when(i >= 1)
    def _():
        step(i - 1, bias_ref[0, 0])

    step(i, bias_ref[0, 1])

    lam = _lambda_from(lv_ref[...])
    o1 = acc1_ref[...] * (1.0 / ml_ref[1:2])
    o2 = acc2_ref[...] * (1.0 / ml_ref[3:4])
    o = o1 - lam * o2
    y = o * lax.rsqrt(jnp.mean(o * o, axis=0, keepdims=True) + SUBLN_EPS)
    y = y * g_ref[...] * (1.0 - LAM_INIT)
    o_ref[0] = y.T


def _prompt_bias_tiles(rel_bias):
    blk = ATTN_BLOCK
    kk = np.arange(blk)[:, None]
    qq = np.arange(blk)[None, :]
    idx = np.stack([_rel_bucket_np(kk - blk - qq), _rel_bucket_np(kk - qq)])
    visible = np.stack([np.ones((blk, blk), bool), (kk // CHUNK) <= (qq // CHUNK)])
    table = rel_bias.T - rel_bias[FAR_BUCKET][:, None]
    tiles = jnp.take(table, jnp.asarray(idx), axis=1)
    return jnp.where(jnp.asarray(visible)[None], tiles, MASKED)


def _prompt_attention(qt, kb, vt, bias_tiles, lam_vecs, subln_col):
    b, nh, nb, _, blk = qt.shape
    t = nb * blk
    return pl.pallas_call(
        _flash_kernel,
        grid=(b, nh, nb),
        in_specs=[
            pl.BlockSpec((1, 1, 1, DV, blk), lambda bi, h, i: (bi, h, i, 0, 0)),
            pl.BlockSpec((1, 1, nb, blk, DV), lambda bi, h, i: (bi, h, 0, 0, 0)),
            pl.BlockSpec((1, 1, nb, DV, blk), lambda bi, h, i: (bi, h, 0, 0, 0)),
            pl.BlockSpec((1, 2, blk, blk), lambda bi, h, i: (h, 0, 0, 0)),
            pl.BlockSpec((4, HEAD_DIM), lambda bi, h, i: (0, 0)),
            pl.BlockSpec((DV, 1), lambda bi, h, i: (0, 0)),
        ],
        out_specs=pl.BlockSpec((1, blk, DV), lambda bi, h, i: (bi, i, h)),
        out_shape=jax.ShapeDtypeStruct((b, t, nh * DV), F32),
        scratch_shapes=[pltpu.VMEM((8, blk), F32),
                        pltpu.VMEM((DV, blk), F32),
                        pltpu.VMEM((DV, blk), F32)],
        compiler_params=pltpu.CompilerParams(
            dimension_semantics=("arbitrary", "arbitrary", "arbitrary"),
            vmem_limit_bytes=VMEM_LIMIT),
        name="prompt_attention",
    )(qt, kb, vt, bias_tiles, lam_vecs, subln_col)


def _sample_attn_kernel(q_ref, ck_ref, cv_ref, kn_ref, vn_ref, bc_ref, bn_ref, lv_ref, g_ref, o_ref):
    lam = _lambda_from(lv_ref[...])
    q = q_ref[0]
    kn = kn_ref[0]
    vn = vn_ref[0]
    lane = lax.broadcasted_iota(jnp.int32, (q.shape[0], DV), 1)
    nt = (((1,), (1,)), ((), ()))

    for hd in range(N_HEADS):
        hs = slice(hd * DV, (hd + 1) * DV)
        qh = q[:, hs]
        kc = ck_ref[0, :, hs].astype(BF16)
        vc = cv_ref[0, :, hs].astype(BF16)
        knh = kn[:, hs].astype(BF16)
        vnh = vn[:, hs].astype(BF16)
        bias_c = bc_ref[hd]
        bias_n = bn_ref[hd]
        outs = []
        for first in (True, False):
            qm = jnp.where((lane < HEAD_DIM) == first, qh, 0.0).astype(BF16)
            sc = lax.dot_general(qm, kc, nt, preferred_element_type=F32) + bias_c
            sn = lax.dot_general(qm, knh, nt, preferred_element_type=F32) + bias_n
            m = jnp.maximum(jnp.max(sc, axis=-1, keepdims=True), jnp.max(sn, axis=-1, keepdims=True))
            pc = jnp.exp(sc - m)
            pn = jnp.exp(sn - m)
            l = jnp.sum(pc, axis=-1, keepdims=True) + jnp.sum(pn, axis=-1, keepdims=True)
            o = (jnp.dot(pc.astype(BF16), vc, preferred_element_type=F32)
                 + jnp.dot(pn.astype(BF16), vnh, preferred_element_type=F32))
            outs.append(o * (1.0 / l))
        o = outs[0] - lam * outs[1]
        y = _rmsnorm(o, g_ref[...], SUBLN_EPS) * (1.0 - LAM_INIT)
        o_ref[0, :, hs] = y


def _sample_bias(rel_bias, past, s_new):
    qpos = past + np.arange(s_new)[:, None]
    idx_c = _rel_bucket_np(np.arange(past)[None, :] - qpos)
    idx_n = _rel_bucket_np(past + np.arange(s_new)[None, :] - qpos)
    return (jnp.take(rel_bias.T, jnp.asarray(idx_c), axis=1),
            jnp.take(rel_bias.T, jnp.asarray(idx_n), axis=1))


def _sample_attention(q, cache_k, cache_v, k_new, v_new, bias_c, bias_n, lam_vecs, subln_row):
    b, s, d = q.shape
    past = cache_k.shape[1]
    new_spec = pl.BlockSpec((1, s, d), lambda i: (i, 0, 0))
    cache_spec = pl.BlockSpec((1, past, d), lambda i: (i, 0, 0))
    return pl.pallas_call(
        _sample_attn_kernel,
        grid=(b,),
        in_specs=[new_spec, cache_spec, cache_spec, new_spec, new_spec,
                  pl.BlockSpec((N_HEADS, s, past), lambda i: (0, 0, 0)),
                  pl.BlockSpec((N_HEADS, s, s), lambda i: (0, 0, 0)),
                  pl.BlockSpec((4, HEAD_DIM), lambda i: (0, 0)),
                  pl.BlockSpec((1, DV), lambda i: (0, 0))],
        out_specs=new_spec,
        out_shape=jax.ShapeDtypeStruct((b, s, d), F32),
        compiler_params=pltpu.CompilerParams(
            dimension_semantics=("arbitrary",), vmem_limit_bytes=VMEM_LIMIT),
        name="sample_attention",
    )(q, cache_k, cache_v, k_new, v_new, bias_c, bias_n, lam_vecs, subln_row)


def _ffn_kernel(x_ref, a_ref, hist_ref, nm_ref, wug_ref, bg_ref, wp_ref, ps_ref, wo_ref,
                nf_ref, wgu_ref, wd_ref, nfin_ref, y_ref, pst_ref, carry_ref, ext_ref,
                *, streams, rows, pos0):
    t = pl.program_id(1)
    n = streams * rows
    d = D_MODEL

    @pl.when(t == 0)
    def _():
        carry_ref[...] = hist_ref[...]

    x = x_ref[...].reshape(n, d)
    h = _rmsnorm(x, nm_ref[...], EPS).astype(BF16)
    u = jnp.dot(h, wug_ref[:, 0:d], preferred_element_type=F32)
    zg = jnp.dot(h, wug_ref[:, d:3 * d], preferred_element_type=F32) + bg_ref[...]

    ext_ref[:, 0:HALO, :] = carry_ref[...]
    ext_ref[:, HALO:HALO + rows, :] = u.reshape(streams, rows, d)
    last = ext_ref[:, rows:rows + HALO, :]
    carry_ref[...] = last
    pst_ref[...] = last

    pos = pos0 + t * rows + lax.broadcasted_iota(jnp.int32, (1, rows, 1), 1)
    pooled = []
    for gi, w in enumerate(POOL_WINDOWS):
        cs = slice(gi * POOL_GROUP, (gi + 1) * POOL_GROUP)
        cur = ext_ref[:, HALO:HALO + rows, cs]
        s = cur
        for back in range(1, w):
            s = s + ext_ref[:, HALO - back:HALO - back + rows, cs]
        inv = 1.0 / jnp.minimum(pos + 1, w).astype(F32)
        dlt = (s * inv - cur).reshape(n, POOL_GROUP).astype(BF16)
        pooled.append(jnp.dot(dlt, wp_ref[gi], preferred_element_type=F32))
    pool = jnp.concatenate(pooled, axis=-1) * ps_ref[...]

    g_a = jax.nn.sigmoid(zg[:, 0:d])
    g_p = jax.nn.sigmoid(zg[:, d:2 * d])
    merged = g_a * a_ref[...].reshape(n, d) + g_p * pool
    x1 = x + jnp.dot(merged.astype(BF16), wo_ref[...], preferred_element_type=F32)

    h2 = _rmsnorm(x1, nf_ref[...], EPS).astype(BF16)
    gu = jnp.dot(h2, wgu_ref[...], preferred_element_type=F32)
    act = jax.nn.silu(gu[:, 0:D_FF]) * gu[:, D_FF:2 * D_FF]
    x2 = x1 + jnp.dot(act.astype(BF16), wd_ref[...], preferred_element_type=F32)
    y_ref[...] = _rmsnorm(x2, nfin_ref[...], EPS).reshape(streams, rows, d)


def _merge_ffn(x, attn, hist, weights, *, streams, rows, pos0):
    b, t, d = x.shape
    tok_spec = pl.BlockSpec((streams, rows, d), lambda i, j: (i, j, 0))
    halo_spec = pl.BlockSpec((streams, HALO, d), lambda i, j: (i, 0, 0))
    return pl.pallas_call(
        functools.partial(_ffn_kernel, streams=streams, rows=rows, pos0=pos0),
        grid=(b // streams, t // rows),
        in_specs=[tok_spec, tok_spec, halo_spec] + [_const_spec(w.shape) for w in weights],
        out_specs=[tok_spec, halo_spec],
        out_shape=[jax.ShapeDtypeStruct((b, t, d), F32),
                   jax.ShapeDtypeStruct((b, HALO, d), F32)],
        scratch_shapes=[pltpu.VMEM((streams, HALO, d), F32),
                        pltpu.VMEM((streams, HALO + rows, d), F32)],
        compiler_params=pltpu.CompilerParams(
            dimension_semantics=("arbitrary", "arbitrary"), vmem_limit_bytes=VMEM_LIMIT),
        name="merge_ffn",
    )(x, attn, hist, *weights)


def kernel(x_prompt, x_sample, cache_k, cache_v, state_pool, rel_bias, norm_mix, w_in, b_gate,
           lambda_q1, lambda_k1, lambda_q2, lambda_k2, subln_g, w_pool, pool_scale, w_out,
           norm_ffn, w_gate_up, w_down, norm_final):
    assert norm_mix.shape[0] == 1, "single layer"
    d = D_MODEL
    bp, tp, _ = x_prompt.shape
    bs, ts, _ = x_sample.shape
    past = cache_k.shape[2]
    assert tp % PROJ_ROWS == 0 and tp % FFN_ROWS == 0 and PROJ_ROWS % ATTN_BLOCK == 0
    assert ATTN_BLOCK % CHUNK == 0 and ts >= HALO and ts % 8 == 0
    assert (past + ts - 1) // CHUNK <= past // CHUNK

    w_in_b = w_in[0].astype(BF16)
    w_qkv = w_in_b[:, 0:3 * d]
    ffn_weights = (norm_mix, w_in_b[:, 3 * d:6 * d], b_gate, w_pool[0].astype(BF16), pool_scale,
                   w_out[0].astype(BF16), norm_ffn, w_gate_up[0].astype(BF16),
                   w_down[0].astype(BF16), norm_final.reshape(1, d))
    lam_vecs = jnp.concatenate([lambda_q1, lambda_k1, lambda_q2, lambda_k2], axis=0)

    k_p, v_p, kb, qt, vt = _project(x_prompt, norm_mix, w_qkv, rows=PROJ_ROWS, blocked=True)
    attn_p = _prompt_attention(qt, kb, vt, _prompt_bias_tiles(rel_bias), lam_vecs,
                               subln_g.reshape(DV, 1))
    y_p, pool_p = _merge_ffn(x_prompt, attn_p, jnp.zeros((bp, HALO, d), F32), ffn_weights,
                             streams=1, rows=FFN_ROWS, pos0=0)

    xs_flat = x_sample.reshape(1, bs * ts, d)
    q_s, k_s, v_s = _project(xs_flat, norm_mix, w_qkv, rows=bs * ts, blocked=False)
    q_s, k_s, v_s = (a.reshape(bs, ts, d) for a in (q_s, k_s, v_s))
    bias_c, bias_n = _sample_bias(rel_bias, past, ts)
    attn_s = _sample_attention(q_s, cache_k[0].reshape(bs, past, d), cache_v[0].reshape(bs, past, d),
                               k_s, v_s, bias_c, bias_n, lam_vecs, subln_g.reshape(1, DV))
    hist_s = jnp.pad(state_pool[0], ((0, 0), (HALO - POOL_HIST, 0), (0, 0)))
    y_s, pool_s = _merge_ffn(x_sample, attn_s, hist_s, ffn_weights, streams=bs, rows=ts, pos0=past)

    heads = (N_HEADS, DV)
    return (y_p, y_s,
            k_p.reshape(1, bp, tp, *heads), v_p.reshape(1, bp, tp, *heads),
            pool_p[None, :, HALO - POOL_HIST:, :],
            k_s.reshape(1, bs, ts, *heads), v_s.reshape(1, bs, ts, *heads),
            pool_s[None, :, HALO - POOL_HIST:, :])
```

```python
import functools
import math

import numpy as np
import jax
import jax.numpy as jnp
from jax import lax
from jax.experimental import pallas as pl
from jax.experimental.pallas import tpu as pltpu

F32 = jnp.float32
BF16 = jnp.bfloat16

D_MODEL = 1024
CHUNK = 64
HEAD_DIM = 64
DV = 2 * HEAD_DIM
N_HEADS = D_MODEL // DV
POOL_WINDOWS = (2, 4, 8, 16)
POOL_GROUP = D_MODEL // len(POOL_WINDOWS)
POOL_HIST = max(POOL_WINDOWS) - 1
HALO = POOL_HIST + 1
D_FF = ((8 * D_MODEL) // 3 + 255) // 256 * 256
NUM_BUCKETS = 32
MAX_DISTANCE = 128
EPS = 1e-6
SUBLN_EPS = 1e-5
LAM_INIT = 0.8 - 0.6 * math.exp(-0.3 * 0)
LOG2E = math.log2(math.e)
QK_SCALE_LOG2 = HEAD_DIM ** -0.5 * LOG2E
MASKED = -1e30

ATTN_BLOCK = 512
ATTN_HEADS = 2
PROJ_ROWS = 512
FFN_ROWS = 256
SAMPLE_NEAR = 128
VMEM_LIMIT = 56 * 1024 * 1024


def _rmsnorm(x, g, eps):
    return x * lax.rsqrt(jnp.mean(x * x, axis=-1, keepdims=True) + eps) * g


def _const_spec(shape, index=None):
    index = (0,) * len(shape) if index is None else index
    return pl.BlockSpec(shape, lambda *_: index, pipeline_mode=pl.Buffered(1))


def _proj_kernel(x_ref, g_ref, w_ref, *out_refs, rows, blocked):
    x = x_ref[0]
    h = _rmsnorm(x, g_ref[...], EPS).astype(BF16)
    q = jnp.dot(h, w_ref[:, 0:D_MODEL], preferred_element_type=F32) * QK_SCALE_LOG2
    k = jnp.dot(h, w_ref[:, D_MODEL:2 * D_MODEL], preferred_element_type=F32)
    v = jnp.dot(h, w_ref[:, 2 * D_MODEL:3 * D_MODEL], preferred_element_type=F32)
    if not blocked:
        q_ref, k_ref, v_ref = out_refs
        q_ref[0] = q
        k_ref[0] = k
        v_ref[0] = v
        return
    k_ref, v_ref, kb_ref, qt_ref, vt_ref = out_refs
    k_ref[0] = k
    v_ref[0] = v
    qt = q.T
    vt = v.T
    for hd in range(N_HEADS):
        hs = slice(hd * DV, (hd + 1) * DV)
        for c in range(rows // ATTN_BLOCK):
            cs = slice(c * ATTN_BLOCK, (c + 1) * ATTN_BLOCK)
            kb_ref[0, hd, c] = k[cs, hs].astype(BF16)
            qt_ref[0, hd, c] = qt[hs, cs].astype(BF16)
            vt_ref[0, hd, c] = vt[hs, cs].astype(BF16)


def _project(x, norm_mix, w_in_b, *, rows, blocked):
    b, t, d = x.shape
    nb_step = rows // ATTN_BLOCK
    nat = jax.ShapeDtypeStruct((b, t, d), F32)
    nat_spec = pl.BlockSpec((1, rows, d), lambda i, j: (i, j, 0))
    if blocked:
        nb = t // ATTN_BLOCK
        out_shape = [nat, nat,
                     jax.ShapeDtypeStruct((b, N_HEADS, nb, ATTN_BLOCK, DV), BF16),
                     jax.ShapeDtypeStruct((b, N_HEADS, nb, DV, ATTN_BLOCK), BF16),
                     jax.ShapeDtypeStruct((b, N_HEADS, nb, DV, ATTN_BLOCK), BF16)]
        out_specs = [nat_spec, nat_spec,
                     pl.BlockSpec((1, N_HEADS, nb_step, ATTN_BLOCK, DV), lambda i, j: (i, 0, j, 0, 0)),
                     pl.BlockSpec((1, N_HEADS, nb_step, DV, ATTN_BLOCK), lambda i, j: (i, 0, j, 0, 0)),
                     pl.BlockSpec((1, N_HEADS, nb_step, DV, ATTN_BLOCK), lambda i, j: (i, 0, j, 0, 0))]
    else:
        out_shape = [nat, nat, nat]
        out_specs = [nat_spec, nat_spec, nat_spec]
    return pl.pallas_call(
        functools.partial(_proj_kernel, rows=rows, blocked=blocked),
        grid=(b, t // rows),
        in_specs=[nat_spec, _const_spec((1, d)), _const_spec((d, 3 * d))],
        out_specs=out_specs,
        out_shape=out_shape,
        compiler_params=pltpu.CompilerParams(
            dimension_semantics=("arbitrary", "arbitrary"), vmem_limit_bytes=VMEM_LIMIT),
        name="qkv_proj_blocked" if blocked else "qkv_proj",
    )(x, norm_mix, w_in_b)


def _rel_bucket_np(rel):
    nb = NUM_BUCKETS // 2
    ret = np.where(rel > 0, nb, 0)
    n = np.abs(rel)
    max_exact = nb // 2
    large = max_exact + (np.log(np.maximum(n, 1).astype(np.float64) / max_exact)
                         / math.log(MAX_DISTANCE / max_exact) * (nb - max_exact)).astype(np.int64)
    large = np.minimum(large, nb - 1)
    return (ret + np.where(n < max_exact, n, large)).astype(np.int32)


FAR_BUCKET = int(_rel_bucket_np(np.array([-MAX_DISTANCE]))[0])
FAR_DISTANCE = int(np.min(np.nonzero(_rel_bucket_np(-np.arange(4 * MAX_DISTANCE)) == FAR_BUCKET)[0]))
assert np.all(_rel_bucket_np(-np.arange(FAR_DISTANCE, 1 << 16)) == FAR_BUCKET)


def _toeplitz(vec, rows, cols):
    h, length = vec.shape
    assert length == rows + cols - 1
    ext = jnp.concatenate([vec, jnp.zeros((h, 1), vec.dtype)], axis=1)
    skew = jnp.tile(ext, (1, rows))[:, :rows * length].reshape(h, rows, length)
    return skew[:, :, rows - 1:rows - 1 + cols]


def _bias_rows(rel_bias, rels):
    table = (rel_bias - rel_bias[FAR_BUCKET][None, :]).T * LOG2E
    return jnp.take(table, jnp.asarray(_rel_bucket_np(np.asarray(rels))), axis=1)


def _lambda_from(lv):
    s1 = jnp.sum(lv[0:1] * lv[1:2], axis=-1, keepdims=True)
    s2 = jnp.sum(lv[2:3] * lv[3:4], axis=-1, keepdims=True)
    return jnp.exp(s1) - jnp.exp(s2) + LAM_INIT


def _flash_kernel(qt_ref, kb_ref, vt_ref, bias_ref, lv_ref, g_ref, o_ref,
                  w_ref, ml_ref, acc_ref, *, blk, heads):
    i = pl.program_id(2)
    zeros = jnp.zeros((HEAD_DIM, blk), BF16)
    for hd in range(heads):
        qt = qt_ref[0, hd, 0]
        w_ref[hd, 0] = jnp.concatenate([qt[:HEAD_DIM], zeros], axis=0)
        w_ref[hd, 1] = jnp.concatenate([zeros, qt[HEAD_DIM:]], axis=0)
        for mp in range(2):
            ml_ref[hd, 2 * mp:2 * mp + 1] = jnp.full((1, blk), MASKED, F32)
            ml_ref[hd, 2 * mp + 1:2 * mp + 2] = jnp.zeros((1, blk), F32)
    acc_ref[...] = jnp.zeros_like(acc_ref)

    def step(j, tile):
        for hd in range(heads):
            kj = kb_ref[0, hd, j]
            vj = vt_ref[0, hd, j]
            for mp in range(2):
                s = jnp.dot(kj, w_ref[hd, mp], preferred_element_type=F32)
                if tile is not None:
                    s = s + bias_ref[hd, tile]
                m_old = ml_ref[hd, 2 * mp:2 * mp + 1]
                l_old = ml_ref[hd, 2 * mp + 1:2 * mp + 2]
                m_new = jnp.maximum(m_old, jnp.max(s, axis=0, keepdims=True))
                alpha = jnp.exp2(m_old - m_new)
                p = jnp.exp2(s - m_new)
                ml_ref[hd, 2 * mp:2 * mp + 1] = m_new
                ml_ref[hd, 2 * mp + 1:2 * mp + 2] = alpha * l_old + jnp.sum(p, axis=0, keepdims=True)
                pv = jnp.dot(vj, p.astype(BF16), preferred_element_type=F32)
                acc_ref[hd, mp] = alpha * acc_ref[hd, mp] + pv

    @pl.loop(0, jnp.maximum(i - 1, 0))
    def _(j):
        step(j, None)

    @pl.when(i >= 1)
    def _():
        step(i - 1, 0)

    step(i, 1)

    lam = _lambda_from(lv_ref[...])
    for hd in range(heads):
        o1 = acc_ref[hd, 0] * (1.0 / ml_ref[hd, 1:2])
        o2 = acc_ref[hd, 1] * (1.0 / ml_ref[hd, 3:4])
        o = o1 - lam * o2
        y = o * lax.rsqrt(jnp.mean(o * o, axis=0, keepdims=True) + SUBLN_EPS)
        y = y * g_ref[...] * (1.0 - LAM_INIT)
        o_ref[0, :, hd * DV:(hd + 1) * DV] = y.T


def _prompt_bias_tiles(rel_bias):
    blk = ATTN_BLOCK
    assert blk >= FAR_DISTANCE
    j = np.arange(2 * blk - 1)
    prev = _toeplitz(_bias_rows(rel_bias, -1 - j), blk, blk)
    diag = _toeplitz(_bias_rows(rel_bias, blk - 1 - j), blk, blk)
    kk = np.arange(blk)[:, None]
    qq = np.arange(blk)[None, :]
    visible = jnp.asarray((kk // CHUNK) <= (qq // CHUNK))
    return jnp.stack([prev, jnp.where(visible[None], diag, MASKED)], axis=1)


def _prompt_attention(qt, kb, vt, bias_tiles, lam_vecs, subln_col):
    b, nh, nb, _, blk = qt.shape
    t = nb * blk
    hps = ATTN_HEADS
    return pl.pallas_call(
        functools.partial(_flash_kernel, blk=blk, heads=hps),
        grid=(b, nh // hps, nb),
        in_specs=[
            pl.BlockSpec((1, hps, 1, DV, blk), lambda bi, h, i: (bi, h, i, 0, 0)),
            pl.BlockSpec((1, hps, nb, blk, DV), lambda bi, h, i: (bi, h, 0, 0, 0)),
            pl.BlockSpec((1, hps, nb, DV, blk), lambda bi, h, i: (bi, h, 0, 0, 0)),
            pl.BlockSpec((hps, 2, blk, blk), lambda bi, h, i: (h, 0, 0, 0),
                         pipeline_mode=pl.Buffered(1)),
            _const_spec((4, HEAD_DIM)),
            _const_spec((DV, 1)),
        ],
        out_specs=pl.BlockSpec((1, blk, hps * DV), lambda bi, h, i: (bi, i, h)),
        out_shape=jax.ShapeDtypeStruct((b, t, nh * DV), F32),
        scratch_shapes=[pltpu.VMEM((hps, 2, DV, blk), BF16),
                        pltpu.VMEM((hps, 8, blk), F32),
                        pltpu.VMEM((hps, 2, DV, blk), F32)],
        compiler_params=pltpu.CompilerParams(
            dimension_semantics=("arbitrary", "arbitrary", "arbitrary"),
            vmem_limit_bytes=VMEM_LIMIT),
        name="prompt_attention",
    )(qt, kb, vt, bias_tiles, lam_vecs, subln_col)


def _sample_attn_kernel(q_ref, ck_ref, cv_ref, kn_ref, vn_ref, bc_ref, bn_ref, lv_ref, g_ref, o_ref,
                        *, far):
    lam = _lambda_from(lv_ref[...])
    q = q_ref[0]
    kn = kn_ref[0]
    vn = vn_ref[0]
    lane = lax.broadcasted_iota(jnp.int32, (q.shape[0], DV), 1)
    nt = (((1,), (1,)), ((), ()))

    for hd in range(N_HEADS):
        hs = slice(hd * DV, (hd + 1) * DV)
        qh = q[:, hs]
        kc = ck_ref[0, :, hs].astype(BF16)
        vc = cv_ref[0, :, hs].astype(BF16)
        knh = kn[:, hs].astype(BF16)
        vnh = vn[:, hs].astype(BF16)
        outs = []
        for first in (True, False):
            qm = jnp.where((lane < HEAD_DIM) == first, qh, 0.0).astype(BF16)
            sf = lax.dot_general(qm, kc[:far], nt, preferred_element_type=F32)
            sc = lax.dot_general(qm, kc[far:], nt, preferred_element_type=F32) + bc_ref[hd]
            sn = lax.dot_general(qm, knh, nt, preferred_element_type=F32) + bn_ref[hd]
            m = jnp.maximum(jnp.maximum(jnp.max(sf, axis=-1, keepdims=True),
                                        jnp.max(sc, axis=-1, keepdims=True)),
                            jnp.max(sn, axis=-1, keepdims=True))
            pf = jnp.exp2(sf - m)
            pc = jnp.exp2(sc - m)
            pn = jnp.exp2(sn - m)
            l = (jnp.sum(pf, axis=-1, keepdims=True) + jnp.sum(pc, axis=-1, keepdims=True)
                 + jnp.sum(pn, axis=-1, keepdims=True))
            o = (jnp.dot(pf.astype(BF16), vc[:far], preferred_element_type=F32)
                 + jnp.dot(pc.astype(BF16), vc[far:], preferred_element_type=F32)
                 + jnp.dot(pn.astype(BF16), vnh, preferred_element_type=F32))
            outs.append(o * (1.0 / l))
        o = outs[0] - lam * outs[1]
        y = _rmsnorm(o, g_ref[...], SUBLN_EPS) * (1.0 - LAM_INIT)
        o_ref[0, :, hs] = y


def _sample_bias(rel_bias, past, s_new):
    near = SAMPLE_NEAR
    j = np.arange(s_new + near - 1)
    bias_c = _toeplitz(_bias_rows(rel_bias, j - (s_new - 1) - near), s_new, near)
    j = np.arange(2 * s_new - 1)
    bias_n = _toeplitz(_bias_rows(rel_bias, j - (s_new - 1)), s_new, s_new)
    return bias_c, bias_n


def _sample_attention(q, cache_k, cache_v, k_new, v_new, bias_c, bias_n, lam_vecs, subln_row):
    b, s, d = q.shape
    past = cache_k.shape[1]
    far = past - SAMPLE_NEAR
    assert far >= 0 and SAMPLE_NEAR >= FAR_DISTANCE
    new_spec = pl.BlockSpec((1, s, d), lambda i: (i, 0, 0))
    cache_spec = pl.BlockSpec((1, past, d), lambda i: (i, 0, 0))
    return pl.pallas_call(
        functools.partial(_sample_attn_kernel, far=far),
        grid=(b,),
        in_specs=[new_spec, cache_spec, cache_spec, new_spec, new_spec,
                  _const_spec(bias_c.shape), _const_spec(bias_n.shape),
                  _const_spec((4, HEAD_DIM)), _const_spec((1, DV))],
        out_specs=new_spec,
        out_shape=jax.ShapeDtypeStruct((b, s, d), F32),
        compiler_params=pltpu.CompilerParams(
            dimension_semantics=("arbitrary",), vmem_limit_bytes=VMEM_LIMIT),
        name="sample_attention",
    )(q, cache_k, cache_v, k_new, v_new, bias_c, bias_n, lam_vecs, subln_row)


def _ffn_kernel(x_ref, a_ref, hist_ref, nm_ref, wug_ref, bg_ref, wp_ref, ps_ref, wo_ref,
                nf_ref, wgu_ref, wd_ref, nfin_ref, y_ref, pst_ref, carry_ref, ext_ref,
                *, streams, rows, pos0):
    t = pl.program_id(1)
    n = streams * rows
    d = D_MODEL

    @pl.when(t == 0)
    def _():
        carry_ref[...] = hist_ref[...]

    x = x_ref[...].reshape(n, d)
    h = _rmsnorm(x, nm_ref[...], EPS).astype(BF16)
    u = jnp.dot(h, wug_ref[:, 0:d], preferred_element_type=F32)
    zg = jnp.dot(h, wug_ref[:, d:3 * d], preferred_element_type=F32) + bg_ref[...]

    ext_ref[:, 0:HALO, :] = carry_ref[...]
    ext_ref[:, HALO:HALO + rows, :] = u.reshape(streams, rows, d)
    last = ext_ref[:, rows:rows + HALO, :]
    carry_ref[...] = last
    pst_ref[...] = last

    pos = pos0 + t * rows + lax.broadcasted_iota(jnp.int32, (1, rows, 1), 1)
    pooled = []
    for gi, w in enumerate(POOL_WINDOWS):
        cs = slice(gi * POOL_GROUP, (gi + 1) * POOL_GROUP)
        cur = ext_ref[:, HALO:HALO + rows, cs]
        s = cur
        for back in range(1, w):
            s = s + ext_ref[:, HALO - back:HALO - back + rows, cs]
        inv = 1.0 / jnp.minimum(pos + 1, w).astype(F32)
        dlt = (s * inv - cur).reshape(n, POOL_GROUP).astype(BF16)
        pooled.append(jnp.dot(dlt, wp_ref[gi], preferred_element_type=F32))
    pool = jnp.concatenate(pooled, axis=-1) * ps_ref[...]

    g_a = jax.nn.sigmoid(zg[:, 0:d])
    g_p = jax.nn.sigmoid(zg[:, d:2 * d])
    merged = g_a * a_ref[...].reshape(n, d) + g_p * pool
    x1 = x + jnp.dot(merged.astype(BF16), wo_ref[...], preferred_element_type=F32)

    h2 = _rmsnorm(x1, nf_ref[...], EPS).astype(BF16)
    gu = jnp.dot(h2, wgu_ref[...], preferred_element_type=F32)
    act = jax.nn.silu(gu[:, 0:D_FF]) * gu[:, D_FF:2 * D_FF]
    x2 = x1 + jnp.dot(act.astype(BF16), wd_ref[...], preferred_element_type=F32)
    y_ref[...] = _rmsnorm(x2, nfin_ref[...], EPS).reshape(streams, rows, d)


def _merge_ffn(x, attn, hist, weights, *, streams, rows, pos0):
    b, t, d = x.shape
    tok_spec = pl.BlockSpec((streams, rows, d), lambda i, j: (i, j, 0))
    halo_spec = pl.BlockSpec((streams, HALO, d), lambda i, j: (i, 0, 0))
    w_specs = [_const_spec(w.shape) for w in weights]
    w_specs[1] = _const_spec((d, 3 * d), index=(0, 1))
    return pl.pallas_call(
        functools.partial(_ffn_kernel, streams=streams, rows=rows, pos0=pos0),
        grid=(b // streams, t // rows),
        in_specs=[tok_spec, tok_spec, halo_spec] + w_specs,
        out_specs=[tok_spec, halo_spec],
        out_shape=[jax.ShapeDtypeStruct((b, t, d), F32),
                   jax.ShapeDtypeStruct((b, HALO, d), F32)],
        scratch_shapes=[pltpu.VMEM((streams, HALO, d), F32),
                        pltpu.VMEM((streams, HALO + rows, d), F32)],
        compiler_params=pltpu.CompilerParams(
            dimension_semantics=("arbitrary", "arbitrary"), vmem_limit_bytes=VMEM_LIMIT),
        name="merge_ffn",
    )(x, attn, hist, *weights)


def kernel(x_prompt, x_sample, cache_k, cache_v, state_pool, rel_bias, norm_mix, w_in, b_gate,
           lambda_q1, lambda_k1, lambda_q2, lambda_k2, subln_g, w_pool, pool_scale, w_out,
           norm_ffn, w_gate_up, w_down, norm_final):
    assert norm_mix.shape[0] == 1, "single layer"
    d = D_MODEL
    bp, tp, _ = x_prompt.shape
    bs, ts, _ = x_sample.shape
    past = cache_k.shape[2]
    assert tp % PROJ_ROWS == 0 and tp % FFN_ROWS == 0 and PROJ_ROWS % ATTN_BLOCK == 0
    assert ATTN_BLOCK % CHUNK == 0 and N_HEADS % ATTN_HEADS == 0 and ts >= HALO and ts % 8 == 0
    assert (past + ts - 1) // CHUNK <= past // CHUNK

    w_in_b = w_in.reshape(d, 6 * d).astype(BF16)
    ffn_weights = (norm_mix, w_in_b, b_gate, w_pool[0].astype(BF16), pool_scale,
                   w_out[0].astype(BF16), norm_ffn, w_gate_up[0].astype(BF16),
                   w_down[0].astype(BF16), norm_final.reshape(1, d))
    lam_vecs = jnp.concatenate([lambda_q1, lambda_k1, lambda_q2, lambda_k2], axis=0)

    k_p, v_p, kb, qt, vt = _project(x_prompt, norm_mix, w_in_b, rows=PROJ_ROWS, blocked=True)
    attn_p = _prompt_attention(qt, kb, vt, _prompt_bias_tiles(rel_bias), lam_vecs,
                               subln_g.reshape(DV, 1))
    y_p, pool_p = _merge_ffn(x_prompt, attn_p, jnp.zeros((bp, HALO, d), F32), ffn_weights,
                             streams=1, rows=FFN_ROWS, pos0=0)

    xs_flat = x_sample.reshape(1, bs * ts, d)
    q_s, k_s, v_s = _project(xs_flat, norm_mix, w_in_b, rows=bs * ts, blocked=False)
    q_s, k_s, v_s = (a.reshape(bs, ts, d) for a in (q_s, k_s, v_s))
    bias_c, bias_n = _sample_bias(rel_bias, past, ts)
    attn_s = _sample_attention(q_s, cache_k.reshape(bs, past, d), cache_v.reshape(bs, past, d),
                               k_s, v_s, bias_c, bias_n, lam_vecs, subln_g.reshape(1, DV))
    hist_s = jnp.pad(state_pool[0], ((0, 0), (HALO - POOL_HIST, 0), (0, 0)))
    y_s, pool_s = _merge_ffn(x_sample, attn_s, hist_s, ffn_weights, streams=bs, rows=ts, pos0=past)

    heads = (N_HEADS, DV)
    return (y_p, y_s,
            k_p.reshape(1, bp, tp, *heads), v_p.reshape(1, bp, tp, *heads),
            pool_p[None, :, HALO - POOL_HIST:, :],
            k_s.reshape(1, bs, ts, *heads), v_s.reshape(1, bs, ts, *heads),
            pool_s[None, :, HALO - POOL_HIST:, :])
```

```python
import functools
import math

import numpy as np
import jax
import jax.numpy as jnp
from jax import lax
from jax.experimental import pallas as pl
from jax.experimental.pallas import tpu as pltpu

F32 = jnp.float32
BF16 = jnp.bfloat16

D_MODEL = 1024
CHUNK = 64
HEAD_DIM = 64
DV = 2 * HEAD_DIM
N_HEADS = D_MODEL // DV
POOL_WINDOWS = (2, 4, 8, 16)
POOL_GROUP = D_MODEL // len(POOL_WINDOWS)
POOL_HIST = max(POOL_WINDOWS) - 1
HALO = POOL_HIST + 1
D_FF = ((8 * D_MODEL) // 3 + 255) // 256 * 256
NUM_BUCKETS = 32
MAX_DISTANCE = 128
EPS = 1e-6
SUBLN_EPS = 1e-5
LAM_INIT = 0.8 - 0.6 * math.exp(-0.3 * 0)
LOG2E = math.log2(math.e)
QK_SCALE_LOG2 = HEAD_DIM ** -0.5 * LOG2E
MASKED = -1e30

ATTN_BLOCK = 512
ATTN_HEADS = 2
PROJ_ROWS = 512
FFN_ROWS = 256
SAMPLE_NEAR = 128
VMEM_LIMIT = 56 * 1024 * 1024


def _rmsnorm(x, g, eps):
    return x * lax.rsqrt(jnp.mean(x * x, axis=-1, keepdims=True) + eps) * g


def _const_spec(shape, index=None):
    index = (0,) * len(shape) if index is None else index
    return pl.BlockSpec(shape, lambda *_: index, pipeline_mode=pl.Buffered(1))


def _proj_kernel(x_ref, g_ref, w_ref, *out_refs, rows, blocked):
    x = x_ref[0]
    h = _rmsnorm(x, g_ref[...], EPS).astype(BF16)
    q = jnp.dot(h, w_ref[:, 0:D_MODEL], preferred_element_type=F32) * QK_SCALE_LOG2
    k = jnp.dot(h, w_ref[:, D_MODEL:2 * D_MODEL], preferred_element_type=F32)
    v = jnp.dot(h, w_ref[:, 2 * D_MODEL:3 * D_MODEL], preferred_element_type=F32)
    if not blocked:
        q_ref, k_ref, v_ref = out_refs
        q_ref[0] = q
        k_ref[0] = k
        v_ref[0] = v
        return
    k_ref, v_ref, kb_ref, qt_ref, vt_ref = out_refs
    qt = q.T
    vt = v.T
    for hd in range(N_HEADS):
        hs = slice(hd * DV, (hd + 1) * DV)
        k_ref[0, pl.ds(hd, rows, stride=N_HEADS), :] = k[:, hs]
        v_ref[0, pl.ds(hd, rows, stride=N_HEADS), :] = v[:, hs]
        for c in range(rows // ATTN_BLOCK):
            cs = slice(c * ATTN_BLOCK, (c + 1) * ATTN_BLOCK)
            kb_ref[0, hd, c] = k[cs, hs].astype(BF16)
            qt_ref[0, hd, c] = qt[hs, cs].astype(BF16)
            vt_ref[0, hd, c] = vt[hs, cs].astype(BF16)


def _project(x, norm_mix, w_in_b, *, rows, blocked):
    b, t, d = x.shape
    nb_step = rows // ATTN_BLOCK
    nat = jax.ShapeDtypeStruct((b, t, d), F32)
    nat_spec = pl.BlockSpec((1, rows, d), lambda i, j: (i, j, 0))
    if blocked:
        nb = t // ATTN_BLOCK
        per_head = jax.ShapeDtypeStruct((b, t * N_HEADS, DV), F32)
        per_head_spec = pl.BlockSpec((1, rows * N_HEADS, DV), lambda i, j: (i, j, 0))
        out_shape = [per_head, per_head,
                     jax.ShapeDtypeStruct((b, N_HEADS, nb, ATTN_BLOCK, DV), BF16),
                     jax.ShapeDtypeStruct((b, N_HEADS, nb, DV, ATTN_BLOCK), BF16),
                     jax.ShapeDtypeStruct((b, N_HEADS, nb, DV, ATTN_BLOCK), BF16)]
        out_specs = [per_head_spec, per_head_spec,
                     pl.BlockSpec((1, N_HEADS, nb_step, ATTN_BLOCK, DV), lambda i, j: (i, 0, j, 0, 0)),
                     pl.BlockSpec((1, N_HEADS, nb_step, DV, ATTN_BLOCK), lambda i, j: (i, 0, j, 0, 0)),
                     pl.BlockSpec((1, N_HEADS, nb_step, DV, ATTN_BLOCK), lambda i, j: (i, 0, j, 0, 0))]
    else:
        out_shape = [nat, nat, nat]
        out_specs = [nat_spec, nat_spec, nat_spec]
    return pl.pallas_call(
        functools.partial(_proj_kernel, rows=rows, blocked=blocked),
        grid=(b, t // rows),
        in_specs=[nat_spec, _const_spec((1, d)), _const_spec((d, 3 * d))],
        out_specs=out_specs,
        out_shape=out_shape,
        compiler_params=pltpu.CompilerParams(
            dimension_semantics=("arbitrary", "arbitrary"), vmem_limit_bytes=VMEM_LIMIT),
        name="qkv_proj_blocked" if blocked else "qkv_proj",
    )(x, norm_mix, w_in_b)


def _rel_bucket_np(rel):
    nb = NUM_BUCKETS // 2
    ret = np.where(rel > 0, nb, 0)
    n = np.abs(rel)
    max_exact = nb // 2
    large = max_exact + (np.log(np.maximum(n, 1).astype(np.float64) / max_exact)
                         / math.log(MAX_DISTANCE / max_exact) * (nb - max_exact)).astype(np.int64)
    large = np.minimum(large, nb - 1)
    return (ret + np.where(n < max_exact, n, large)).astype(np.int32)


FAR_BUCKET = int(_rel_bucket_np(np.array([-MAX_DISTANCE]))[0])
FAR_DISTANCE = int(np.min(np.nonzero(_rel_bucket_np(-np.arange(4 * MAX_DISTANCE)) == FAR_BUCKET)[0]))
assert np.all(_rel_bucket_np(-np.arange(FAR_DISTANCE, 1 << 16)) == FAR_BUCKET)


def _toeplitz(vec, rows, cols):
    h, length = vec.shape
    assert length == rows + cols - 1
    ext = jnp.concatenate([vec, jnp.zeros((h, 1), vec.dtype)], axis=1)
    skew = jnp.tile(ext, (1, rows))[:, :rows * length].reshape(h, rows, length)
    return skew[:, :, rows - 1:rows - 1 + cols]


def _bias_rows(rel_bias, rels):
    table = (rel_bias - rel_bias[FAR_BUCKET][None, :]).T * LOG2E
    return jnp.take(table, jnp.asarray(_rel_bucket_np(np.asarray(rels))), axis=1)


def _lambda_from(lv):
    s1 = jnp.sum(lv[0:1] * lv[1:2], axis=-1, keepdims=True)
    s2 = jnp.sum(lv[2:3] * lv[3:4], axis=-1, keepdims=True)
    return jnp.exp(s1) - jnp.exp(s2) + LAM_INIT


def _flash_kernel(qt_ref, kb_ref, vt_ref, bias_ref, lv_ref, g_ref, o_ref,
                  w_ref, m_ref, l_ref, acc_ref, *, blk, heads):
    i = pl.program_id(2)
    zeros = jnp.zeros((HEAD_DIM, blk), BF16)
    for hd in range(heads):
        qt = qt_ref[0, hd, 0]
        w_ref[hd, 0] = jnp.concatenate([qt[:HEAD_DIM], zeros], axis=0)
        w_ref[hd, 1] = jnp.concatenate([zeros, qt[HEAD_DIM:]], axis=0)

    def partial_sums(p):
        return jnp.sum(p.reshape(blk // 8, 8, blk), axis=0)

    def fast_step(j, tile, first):
        for hd in range(heads):
            for mp in range(2):
                s = jnp.dot(kb_ref[0, hd, j], w_ref[hd, mp], preferred_element_type=F32)
                if tile is not None:
                    s = s + bias_ref[hd, tile]
                if first:
                    m = jnp.max(s, axis=0, keepdims=True)
                    m_ref[hd, mp] = m
                else:
                    m = m_ref[hd, mp]
                p = jnp.exp2(s - m)
                pv = jnp.dot(vt_ref[0, hd, j], p.astype(BF16), preferred_element_type=F32)
                if first:
                    l_ref[hd, mp] = partial_sums(p)
                    acc_ref[hd, mp] = pv
                else:
                    l_ref[hd, mp] += partial_sums(p)
                    acc_ref[hd, mp] += pv

    fast_step(i, 1, True)

    @pl.when(i >= 1)
    def _():
        fast_step(i - 1, 0, False)

    @pl.loop(0, jnp.maximum(i - 1, 0))
    def _(j):
        fast_step(j, None, False)

    poison = jnp.sum(acc_ref[...] * 0.0) + jnp.sum(l_ref[...] * 0.0)

    @pl.when(jnp.logical_not(poison == 0.0))
    def _():
        @pl.loop(0, 2 * heads)
        def _(c):
            hd = c // 2
            mp = c % 2
            m_ref[hd, mp] = jnp.full((1, blk), MASKED, F32)
            l_ref[hd, mp] = jnp.zeros((8, blk), F32)
            acc_ref[hd, mp] = jnp.zeros((DV, blk), F32)

            @pl.loop(0, i + 1)
            def _(j):
                s = jnp.dot(kb_ref[0, hd, j], w_ref[hd, mp], preferred_element_type=F32)
                tile = jnp.clip(j - (i - 1), 0, 1)
                s = s + jnp.where(j >= i - 1, bias_ref[hd, tile], 0.0)
                m_old = m_ref[hd, mp]
                m_new = jnp.maximum(m_old, jnp.max(s, axis=0, keepdims=True))
                alpha = jnp.exp2(m_old - m_new)
                p = jnp.exp2(s - m_new)
                m_ref[hd, mp] = m_new
                l_ref[hd, mp] = alpha * l_ref[hd, mp] + partial_sums(p)
                pv = jnp.dot(vt_ref[0, hd, j], p.astype(BF16), preferred_element_type=F32)
                acc_ref[hd, mp] = alpha * acc_ref[hd, mp] + pv

    lam = _lambda_from(lv_ref[...])
    for hd in range(heads):
        o1 = acc_ref[hd, 0] * (1.0 / jnp.sum(l_ref[hd, 0], axis=0, keepdims=True))
        o2 = acc_ref[hd, 1] * (1.0 / jnp.sum(l_ref[hd, 1], axis=0, keepdims=True))
        o = o1 - lam * o2
        y = o * lax.rsqrt(jnp.mean(o * o, axis=0, keepdims=True) + SUBLN_EPS)
        y = y * g_ref[...] * (1.0 - LAM_INIT)
        o_ref[0, :, hd * DV:(hd + 1) * DV] = y.T


def _prompt_bias_tiles(rel_bias):
    blk = ATTN_BLOCK
    assert blk >= FAR_DISTANCE
    j = np.arange(2 * blk - 1)
    prev = _toeplitz(_bias_rows(rel_bias, -1 - j), blk, blk)
    diag = _toeplitz(_bias_rows(rel_bias, blk - 1 - j), blk, blk)
    kk = np.arange(blk)[:, None]
    qq = np.arange(blk)[None, :]
    visible = jnp.asarray((kk // CHUNK) <= (qq // CHUNK))
    return jnp.stack([prev, jnp.where(visible[None], diag, MASKED)], axis=1)


def _prompt_attention(qt, kb, vt, bias_tiles, lam_vecs, subln_col):
    b, nh, nb, _, blk = qt.shape
    t = nb * blk
    hps = ATTN_HEADS
    return pl.pallas_call(
        functools.partial(_flash_kernel, blk=blk, heads=hps),
        grid=(b, nh // hps, nb),
        in_specs=[
            pl.BlockSpec((1, hps, 1, DV, blk), lambda bi, h, i: (bi, h, i, 0, 0)),
            pl.BlockSpec((1, hps, nb, blk, DV), lambda bi, h, i: (bi, h, 0, 0, 0)),
            pl.BlockSpec((1, hps, nb, DV, blk), lambda bi, h, i: (bi, h, 0, 0, 0)),
            pl.BlockSpec((hps, 2, blk, blk), lambda bi, h, i: (h, 0, 0, 0),
                         pipeline_mode=pl.Buffered(1)),
            _const_spec((4, HEAD_DIM)),
            _const_spec((DV, 1)),
        ],
        out_specs=pl.BlockSpec((1, blk, hps * DV), lambda bi, h, i: (bi, i, h)),
        out_shape=jax.ShapeDtypeStruct((b, t, nh * DV), F32),
        scratch_shapes=[pltpu.VMEM((hps, 2, DV, blk), BF16),
                        pltpu.VMEM((hps, 2, 1, blk), F32),
                        pltpu.VMEM((hps, 2, 8, blk), F32),
                        pltpu.VMEM((hps, 2, DV, blk), F32)],
        compiler_params=pltpu.CompilerParams(
            dimension_semantics=("arbitrary", "arbitrary", "arbitrary"),
            vmem_limit_bytes=VMEM_LIMIT),
        name="prompt_attention",
    )(qt, kb, vt, bias_tiles, lam_vecs, subln_col)


def _sample_attn_kernel(q_ref, ck_ref, cv_ref, kn_ref, vn_ref, bc_ref, bn_ref, lv_ref, g_ref, o_ref,
                        *, past, far):
    lam = _lambda_from(lv_ref[...])
    q = q_ref[0]
    kn = kn_ref[0]
    vn = vn_ref[0]
    lane = lax.broadcasted_iota(jnp.int32, (q.shape[0], DV), 1)
    nt = (((1,), (1,)), ((), ()))

    for hd in range(N_HEADS):
        hs = slice(hd * DV, (hd + 1) * DV)
        qh = q[:, hs]
        kc = ck_ref[0, pl.ds(hd, past, stride=N_HEADS), :].astype(BF16)
        vc = cv_ref[0, pl.ds(hd, past, stride=N_HEADS), :].astype(BF16)
        knh = kn[:, hs].astype(BF16)
        vnh = vn[:, hs].astype(BF16)
        outs = []
        for first in (True, False):
            qm = jnp.where((lane < HEAD_DIM) == first, qh, 0.0).astype(BF16)
            sf = lax.dot_general(qm, kc[:far], nt, preferred_element_type=F32)
            sc = lax.dot_general(qm, kc[far:], nt, preferred_element_type=F32) + bc_ref[hd]
            sn = lax.dot_general(qm, knh, nt, preferred_element_type=F32) + bn_ref[hd]
            m = jnp.maximum(jnp.maximum(jnp.max(sf, axis=-1, keepdims=True),
                                        jnp.max(sc, axis=-1, keepdims=True)),
                            jnp.max(sn, axis=-1, keepdims=True))
            pf = jnp.exp2(sf - m)
            pc = jnp.exp2(sc - m)
            pn = jnp.exp2(sn - m)
            l = (jnp.sum(pf, axis=-1, keepdims=True) + jnp.sum(pc, axis=-1, keepdims=True)
                 + jnp.sum(pn, axis=-1, keepdims=True))
            o = (jnp.dot(pf.astype(BF16), vc[:far], preferred_element_type=F32)
                 + jnp.dot(pc.astype(BF16), vc[far:], preferred_element_type=F32)
                 + jnp.dot(pn.astype(BF16), vnh, preferred_element_type=F32))
            outs.append(o * (1.0 / l))
        o = outs[0] - lam * outs[1]
        y = _rmsnorm(o, g_ref[...], SUBLN_EPS) * (1.0 - LAM_INIT)
        o_ref[0, :, hs] = y


def _sample_bias(rel_bias, past, s_new):
    near = SAMPLE_NEAR
    j = np.arange(s_new + near - 1)
    bias_c = _toeplitz(_bias_rows(rel_bias, j - (s_new - 1) - near), s_new, near)
    j = np.arange(2 * s_new - 1)
    bias_n = _toeplitz(_bias_rows(rel_bias, j - (s_new - 1)), s_new, s_new)
    return bias_c, bias_n


def _sample_attention(q, cache_k, cache_v, k_new, v_new, bias_c, bias_n, lam_vecs, subln_row):
    b, s, d = q.shape
    past = cache_k.shape[1] // N_HEADS
    far = past - SAMPLE_NEAR
    assert far >= 0 and SAMPLE_NEAR >= FAR_DISTANCE
    new_spec = pl.BlockSpec((1, s, d), lambda i: (i, 0, 0))
    cache_spec = pl.BlockSpec((1, past * N_HEADS, DV), lambda i: (i, 0, 0))
    return pl.pallas_call(
        functools.partial(_sample_attn_kernel, past=past, far=far),
        grid=(b,),
        in_specs=[new_spec, cache_spec, cache_spec, new_spec, new_spec,
                  _const_spec(bias_c.shape), _const_spec(bias_n.shape),
                  _const_spec((4, HEAD_DIM)), _const_spec((1, DV))],
        out_specs=new_spec,
        out_shape=jax.ShapeDtypeStruct((b, s, d), F32),
        compiler_params=pltpu.CompilerParams(
            dimension_semantics=("arbitrary",), vmem_limit_bytes=VMEM_LIMIT),
        name="sample_attention",
    )(q, cache_k, cache_v, k_new, v_new, bias_c, bias_n, lam_vecs, subln_row)


def _ffn_kernel(x_ref, a_ref, hist_ref, nm_ref, wug_ref, bg_ref, wp_ref, ps_ref, wo_ref,
                nf_ref, wgu_ref, wd_ref, nfin_ref, y_ref, pst_ref, carry_ref, ext_ref,
                *, streams, rows, pos0):
    t = pl.program_id(1)
    n = streams * rows
    d = D_MODEL

    @pl.when(t == 0)
    def _():
        carry_ref[...] = hist_ref[...]

    x = x_ref[...].reshape(n, d)
    h = _rmsnorm(x, nm_ref[...], EPS).astype(BF16)
    u = jnp.dot(h, wug_ref[:, 0:d], preferred_element_type=F32)
    zg = jnp.dot(h, wug_ref[:, d:3 * d], preferred_element_type=F32) + bg_ref[...]

    ext_ref[:, 0:HALO, :] = carry_ref[...]
    ext_ref[:, HALO:HALO + rows, :] = u.reshape(streams, rows, d)
    last = ext_ref[:, rows:rows + HALO, :]
    carry_ref[...] = last
    pst_ref[...] = last

    pos = pos0 + t * rows + lax.broadcasted_iota(jnp.int32, (1, rows, 1), 1)
    pooled = []
    for gi, w in enumerate(POOL_WINDOWS):
        cs = slice(gi * POOL_GROUP, (gi + 1) * POOL_GROUP)
        cur = ext_ref[:, HALO:HALO + rows, cs]
        s = cur
        for back in range(1, w):
            s = s + ext_ref[:, HALO - back:HALO - back + rows, cs]
        inv = 1.0 / jnp.minimum(pos + 1, w).astype(F32)
        dlt = (s * inv - cur).reshape(n, POOL_GROUP).astype(BF16)
        pooled.append(jnp.dot(dlt, wp_ref[gi], preferred_element_type=F32))
    pool = jnp.concatenate(pooled, axis=-1) * ps_ref[...]

    g_a = jax.nn.sigmoid(zg[:, 0:d])
    g_p = jax.nn.sigmoid(zg[:, d:2 * d])
    merged = g_a * a_ref[...].reshape(n, d) + g_p * pool
    x1 = x + jnp.dot(merged.astype(BF16), wo_ref[...], preferred_element_type=F32)

    h2 = _rmsnorm(x1, nf_ref[...], EPS).astype(BF16)
    gu = jnp.dot(h2, wgu_ref[...], preferred_element_type=F32)
    act = jax.nn.silu(gu[:, 0:D_FF]) * gu[:, D_FF:2 * D_FF]
    x2 = x1 + jnp.dot(act.astype(BF16), wd_ref[...], preferred_element_type=F32)
    y_ref[...] = _rmsnorm(x2, nfin_ref[...], EPS).reshape(streams, rows, d)


def _merge_ffn(x, attn, hist, weights, *, streams, rows, pos0):
    b, t, d = x.shape
    tok_spec = pl.BlockSpec((streams, rows, d), lambda i, j: (i, j, 0))
    halo_spec = pl.BlockSpec((streams, HALO, d), lambda i, j: (i, 0, 0))
    w_specs = [_const_spec(w.shape) for w in weights]
    w_specs[1] = _const_spec((d, 3 * d), index=(0, 1))
    return pl.pallas_call(
        functools.partial(_ffn_kernel, streams=streams, rows=rows, pos0=pos0),
        grid=(b // streams, t // rows),
        in_specs=[tok_spec, tok_spec, halo_spec] + w_specs,
        out_specs=[tok_spec, halo_spec],
        out_shape=[jax.ShapeDtypeStruct((b, t, d), F32),
                   jax.ShapeDtypeStruct((b, HALO, d), F32)],
        scratch_shapes=[pltpu.VMEM((streams, HALO, d), F32),
                        pltpu.VMEM((streams, HALO + rows, d), F32)],
        compiler_params=pltpu.CompilerParams(
            dimension_semantics=("arbitrary", "arbitrary"), vmem_limit_bytes=VMEM_LIMIT),
        name="merge_ffn",
    )(x, attn, hist, *weights)


def kernel(x_prompt, x_sample, cache_k, cache_v, state_pool, rel_bias, norm_mix, w_in, b_gate,
           lambda_q1, lambda_k1, lambda_q2, lambda_k2, subln_g, w_pool, pool_scale, w_out,
           norm_ffn, w_gate_up, w_down, norm_final):
    assert norm_mix.shape[0] == 1, "single layer"
    d = D_MODEL
    bp, tp, _ = x_prompt.shape
    bs, ts, _ = x_sample.shape
    past = cache_k.shape[2]
    assert tp % PROJ_ROWS == 0 and tp % FFN_ROWS == 0 and PROJ_ROWS % ATTN_BLOCK == 0
    assert ATTN_BLOCK % CHUNK == 0 and N_HEADS % ATTN_HEADS == 0 and ts >= HALO and ts % 8 == 0
    assert (past + ts - 1) // CHUNK <= past // CHUNK

    w_in_b = w_in.reshape(d, 6 * d).astype(BF16)
    ffn_weights = (norm_mix, w_in_b, b_gate, w_pool[0].astype(BF16), pool_scale,
                   w_out[0].astype(BF16), norm_ffn, w_gate_up[0].astype(BF16),
                   w_down[0].astype(BF16), norm_final.reshape(1, d))
    lam_vecs = jnp.concatenate([lambda_q1, lambda_k1, lambda_q2, lambda_k2], axis=0)

    k_p, v_p, kb, qt, vt = _project(x_prompt, norm_mix, w_in_b, rows=PROJ_ROWS, blocked=True)
    attn_p = _prompt_attention(qt, kb, vt, _prompt_bias_tiles(rel_bias), lam_vecs,
                               subln_g.reshape(DV, 1))
    y_p, pool_p = _merge_ffn(x_prompt, attn_p, jnp.zeros((bp, HALO, d), F32), ffn_weights,
                             streams=1, rows=FFN_ROWS, pos0=0)

    xs_flat = x_sample.reshape(1, bs * ts, d)
    q_s, k_s, v_s = _project(xs_flat, norm_mix, w_in_b, rows=bs * ts, blocked=False)
    q_s, k_s, v_s = (a.reshape(bs, ts, d) for a in (q_s, k_s, v_s))
    bias_c, bias_n = _sample_bias(rel_bias, past, ts)
    attn_s = _sample_attention(q_s, cache_k.reshape(bs, past * N_HEADS, DV),
                               cache_v.reshape(bs, past * N_HEADS, DV), k_s, v_s, bias_c, bias_n, lam_vecs, subln_g.reshape(1, DV))
    hist_s = jnp.pad(state_pool[0], ((0, 0), (HALO - POOL_HIST, 0), (0, 0)))
    y_s, pool_s = _merge_ffn(x_sample, attn_s, hist_s, ffn_weights, streams=bs, rows=ts, pos0=past)

    heads = (N_HEADS, DV)
    return (y_p, y_s,
            k_p.reshape(1, bp, tp, *heads), v_p.reshape(1, bp, tp, *heads),
            pool_p[None, :, HALO - POOL_HIST:, :],
            k_s.reshape(1, bs, ts, *heads), v_s.reshape(1, bs, ts, *heads),
            pool_s[None, :, HALO - POOL_HIST:, :])
```

```python
import functools
import math

import numpy as np
import jax
import jax.numpy as jnp
from jax import lax
from jax.experimental import pallas as pl
from jax.experimental.pallas import tpu as pltpu

F32 = jnp.float32
BF16 = jnp.bfloat16

D_MODEL = 1024
CHUNK = 64
HEAD_DIM = 64
DV = 2 * HEAD_DIM
DVX = DV + 16
N_HEADS = D_MODEL // DV
POOL_WINDOWS = (2, 4, 8, 16)
POOL_GROUP = D_MODEL // len(POOL_WINDOWS)
POOL_HIST = max(POOL_WINDOWS) - 1
HALO = POOL_HIST + 1
D_FF = ((8 * D_MODEL) // 3 + 255) // 256 * 256
NUM_BUCKETS = 32
MAX_DISTANCE = 128
EPS = 1e-6
SUBLN_EPS = 1e-5
LAM_INIT = 0.8 - 0.6 * math.exp(-0.3 * 0)
LOG2E = math.log2(math.e)
QK_SCALE_LOG2 = HEAD_DIM ** -0.5 * LOG2E
MASKED = -1e30

ATTN_BLOCK = 512
ATTN_HEADS = 4
PROJ_ROWS = 512
FFN_ROWS = 256
SAMPLE_NEAR = 128
VMEM_LIMIT = 56 * 1024 * 1024


def _rmsnorm(x, g, eps):
    return x * lax.rsqrt(jnp.mean(x * x, axis=-1, keepdims=True) + eps) * g


def _const_spec(shape, index=None):
    index = (0,) * len(shape) if index is None else index
    return pl.BlockSpec(shape, lambda *_: index, pipeline_mode=pl.Buffered(1))


def _proj_kernel(x_ref, g_ref, w_ref, *out_refs, rows, blocked):
    x = x_ref[0]
    h = _rmsnorm(x, g_ref[...], EPS).astype(BF16)
    q = jnp.dot(h, w_ref[:, 0:D_MODEL], preferred_element_type=F32) * QK_SCALE_LOG2
    k = jnp.dot(h, w_ref[:, D_MODEL:2 * D_MODEL], preferred_element_type=F32)
    v = jnp.dot(h, w_ref[:, 2 * D_MODEL:3 * D_MODEL], preferred_element_type=F32)
    if not blocked:
        q_ref, k_ref, v_ref = out_refs
        q_ref[0] = q
        k_ref[0] = k
        v_ref[0] = v
        return
    k_ref, v_ref, kb_ref, qt_ref, vt_ref = out_refs
    qt = q.T
    vt = v.T
    ones_rows = (lax.broadcasted_iota(jnp.int32, (DVX - DV, ATTN_BLOCK), 0) == 0).astype(BF16)
    for hd in range(N_HEADS):
        hs = slice(hd * DV, (hd + 1) * DV)
        k_ref[0, pl.ds(hd, rows, stride=N_HEADS), :] = k[:, hs]
        v_ref[0, pl.ds(hd, rows, stride=N_HEADS), :] = v[:, hs]
        for c in range(rows // ATTN_BLOCK):
            cs = slice(c * ATTN_BLOCK, (c + 1) * ATTN_BLOCK)
            kb_ref[0, hd, c] = k[cs, hs].astype(BF16)
            qt_ref[0, hd, c] = qt[hs, cs].astype(BF16)
            vt_ref[0, hd, c, 0:DV, :] = vt[hs, cs].astype(BF16)
            vt_ref[0, hd, c, DV:DVX, :] = ones_rows


def _project(x, norm_mix, w_in_b, *, rows, blocked):
    b, t, d = x.shape
    nb_step = rows // ATTN_BLOCK
    nat = jax.ShapeDtypeStruct((b, t, d), F32)
    nat_spec = pl.BlockSpec((1, rows, d), lambda i, j: (i, j, 0))
    if blocked:
        nb = t // ATTN_BLOCK
        per_head = jax.ShapeDtypeStruct((b, t * N_HEADS, DV), F32)
        per_head_spec = pl.BlockSpec((1, rows * N_HEADS, DV), lambda i, j: (i, j, 0))
        out_shape = [per_head, per_head,
                     jax.ShapeDtypeStruct((b, N_HEADS, nb, ATTN_BLOCK, DV), BF16),
                     jax.ShapeDtypeStruct((b, N_HEADS, nb, DV, ATTN_BLOCK), BF16),
                     jax.ShapeDtypeStruct((b, N_HEADS, nb, DVX, ATTN_BLOCK), BF16)]
        out_specs = [per_head_spec, per_head_spec,
                     pl.BlockSpec((1, N_HEADS, nb_step, ATTN_BLOCK, DV), lambda i, j: (i, 0, j, 0, 0)),
                     pl.BlockSpec((1, N_HEADS, nb_step, DV, ATTN_BLOCK), lambda i, j: (i, 0, j, 0, 0)),
                     pl.BlockSpec((1, N_HEADS, nb_step, DVX, ATTN_BLOCK), lambda i, j: (i, 0, j, 0, 0))]
    else:
        out_shape = [nat, nat, nat]
        out_specs = [nat_spec, nat_spec, nat_spec]
    return pl.pallas_call(
        functools.partial(_proj_kernel, rows=rows, blocked=blocked),
        grid=(b, t // rows),
        in_specs=[nat_spec, _const_spec((1, d)), _const_spec((d, 3 * d))],
        out_specs=out_specs,
        out_shape=out_shape,
        compiler_params=pltpu.CompilerParams(
            dimension_semantics=("arbitrary", "arbitrary"), vmem_limit_bytes=VMEM_LIMIT),
        name="qkv_proj_blocked" if blocked else "qkv_proj",
    )(x, norm_mix, w_in_b)


def _rel_bucket_np(rel):
    nb = NUM_BUCKETS // 2
    ret = np.where(rel > 0, nb, 0)
    n = np.abs(rel)
    max_exact = nb // 2
    large = max_exact + (np.log(np.maximum(n, 1).astype(np.float64) / max_exact)
                         / math.log(MAX_DISTANCE / max_exact) * (nb - max_exact)).astype(np.int64)
    large = np.minimum(large, nb - 1)
    return (ret + np.where(n < max_exact, n, large)).astype(np.int32)


FAR_BUCKET = int(_rel_bucket_np(np.array([-MAX_DISTANCE]))[0])
FAR_DISTANCE = int(np.min(np.nonzero(_rel_bucket_np(-np.arange(4 * MAX_DISTANCE)) == FAR_BUCKET)[0]))
assert np.all(_rel_bucket_np(-np.arange(FAR_DISTANCE, 1 << 16)) == FAR_BUCKET)


def _toeplitz(vec, rows, cols):
    h, length = vec.shape
    assert length == rows + cols - 1
    ext = jnp.concatenate([vec, jnp.zeros((h, 1), vec.dtype)], axis=1)
    skew = jnp.tile(ext, (1, rows))[:, :rows * length].reshape(h, rows, length)
    return skew[:, :, rows - 1:rows - 1 + cols]


def _bias_rows(rel_bias, rels):
    table = (rel_bias - rel_bias[FAR_BUCKET][None, :]).T * LOG2E
    return jnp.take(table, jnp.asarray(_rel_bucket_np(np.asarray(rels))), axis=1)


def _lambda_from(lv):
    s1 = jnp.sum(lv[0:1] * lv[1:2], axis=-1, keepdims=True)
    s2 = jnp.sum(lv[2:3] * lv[3:4], axis=-1, keepdims=True)
    return jnp.exp(s1) - jnp.exp(s2) + LAM_INIT


def _flash_kernel(qt_ref, kb_ref, vt_ref, bias_ref, lv_ref, g_ref, o_ref,
                  w_ref, m_ref, acc_ref, *, blk, heads):
    i = pl.program_id(2)
    zeros = jnp.zeros((HEAD_DIM, blk), BF16)
    for hd in range(heads):
        qt = qt_ref[0, hd, 0]
        w_ref[hd, 0] = jnp.concatenate([qt[:HEAD_DIM], zeros], axis=0)
        w_ref[hd, 1] = jnp.concatenate([zeros, qt[HEAD_DIM:]], axis=0)

    def fast_step(j, tile, first):
        chains = [(hd, mp) for hd in range(heads) for mp in range(2)]

        def scores(hd, mp):
            s = jnp.dot(kb_ref[0, hd, j], w_ref[hd, mp], preferred_element_type=F32)
            if tile is not None:
                s = s + bias_ref[hd, tile]
            return s

        def consume(hd, mp, s):
            if first:
                m = jnp.max(s, axis=0, keepdims=True)
                m_ref[hd, mp] = m
            else:
                m = m_ref[hd, mp]
            p = jnp.exp2((s - m).astype(BF16))
            pv = jnp.dot(vt_ref[0, hd, j], p, preferred_element_type=F32)
            if first:
                acc_ref[hd, mp] = pv
            else:
                acc_ref[hd, mp] += pv

        pending = scores(*chains[0])
        for n, (hd, mp) in enumerate(chains):
            nxt = scores(*chains[n + 1]) if n + 1 < len(chains) else None
            consume(hd, mp, pending)
            pending = nxt

    fast_step(i, 1, True)

    @pl.when(i >= 1)
    def _():
        fast_step(i - 1, 0, False)

    @pl.loop(0, jnp.maximum(i - 1, 0))
    def _(j):
        fast_step(j, None, False)

    poison = jnp.sum(acc_ref[...] * 0.0)

    @pl.when(jnp.logical_not(poison == 0.0))
    def _():
        @pl.loop(0, 2 * heads)
        def _(c):
            hd = c // 2
            mp = c % 2
            m_ref[hd, mp] = jnp.full((1, blk), MASKED, F32)
            acc_ref[hd, mp] = jnp.zeros((DVX, blk), F32)

            @pl.loop(0, i + 1)
            def _(j):
                s = jnp.dot(kb_ref[0, hd, j], w_ref[hd, mp], preferred_element_type=F32)
                tile = jnp.clip(j - (i - 1), 0, 1)
                s = s + jnp.where(j >= i - 1, bias_ref[hd, tile], 0.0)
                m_old = m_ref[hd, mp]
                m_new = jnp.maximum(m_old, jnp.max(s, axis=0, keepdims=True))
                alpha = jnp.exp2(m_old - m_new)
                p = jnp.exp2((s - m_new).astype(BF16))
                m_ref[hd, mp] = m_new
                pv = jnp.dot(vt_ref[0, hd, j], p, preferred_element_type=F32)
                acc_ref[hd, mp] = alpha * acc_ref[hd, mp] + pv

    lam = _lambda_from(lv_ref[...])
    for hd in range(heads):
        o1 = acc_ref[hd, 0, 0:DV, :] * (1.0 / acc_ref[hd, 0, DV:DV + 1, :])
        o2 = acc_ref[hd, 1, 0:DV, :] * (1.0 / acc_ref[hd, 1, DV:DV + 1, :])
        o = o1 - lam * o2
        y = o * lax.rsqrt(jnp.mean(o * o, axis=0, keepdims=True) + SUBLN_EPS)
        y = y * g_ref[...] * (1.0 - LAM_INIT)
        o_ref[0, :, hd * DV:(hd + 1) * DV] = y.T


def _prompt_bias_tiles(rel_bias):
    blk = ATTN_BLOCK
    assert blk >= FAR_DISTANCE
    j = np.arange(2 * blk - 1)
    prev = _toeplitz(_bias_rows(rel_bias, -1 - j), blk, blk)
    diag = _toeplitz(_bias_rows(rel_bias, blk - 1 - j), blk, blk)
    kk = np.arange(blk)[:, None]
    qq = np.arange(blk)[None, :]
    visible = jnp.asarray((kk // CHUNK) <= (qq // CHUNK))
    return jnp.stack([prev, jnp.where(visible[None], diag, MASKED)], axis=1)


def _prompt_attention(qt, kb, vt, bias_tiles, lam_vecs, subln_col):
    b, nh, nb, _, blk = qt.shape
    t = nb * blk
    hps = ATTN_HEADS
    return pl.pallas_call(
        functools.partial(_flash_kernel, blk=blk, heads=hps),
        grid=(b, nh // hps, nb),
        in_specs=[
            pl.BlockSpec((1, hps, 1, DV, blk), lambda bi, h, i: (bi, h, i, 0, 0)),
            pl.BlockSpec((1, hps, nb, blk, DV), lambda bi, h, i: (bi, h, 0, 0, 0)),
            pl.BlockSpec((1, hps, nb, DVX, blk), lambda bi, h, i: (bi, h, 0, 0, 0)),
            pl.BlockSpec((hps, 2, blk, blk), lambda bi, h, i: (h, 0, 0, 0),
                         pipeline_mode=pl.Buffered(1)),
            _const_spec((4, HEAD_DIM)),
            _const_spec((DV, 1)),
        ],
        out_specs=pl.BlockSpec((1, blk, hps * DV), lambda bi, h, i: (bi, i, h)),
        out_shape=jax.ShapeDtypeStruct((b, t, nh * DV), F32),
        scratch_shapes=[pltpu.VMEM((hps, 2, DV, blk), BF16),
                        pltpu.VMEM((hps, 2, 1, blk), F32),
                        pltpu.VMEM((hps, 2, DVX, blk), F32)],
        compiler_params=pltpu.CompilerParams(
            dimension_semantics=("arbitrary", "arbitrary", "arbitrary"),
            vmem_limit_bytes=VMEM_LIMIT),
        name="prompt_attention",
    )(qt, kb, vt, bias_tiles, lam_vecs, subln_col)


def _sample_attn_kernel(q_ref, ck_ref, cv_ref, kn_ref, vn_ref, bc_ref, bn_ref, lv_ref, g_ref, o_ref,
                        *, past, far):
    lam = _lambda_from(lv_ref[...])
    q = q_ref[0]
    kn = kn_ref[0]
    vn = vn_ref[0]
    lane = lax.broadcasted_iota(jnp.int32, (q.shape[0], DV), 1)
    nt = (((1,), (1,)), ((), ()))

    for hd in range(N_HEADS):
        hs = slice(hd * DV, (hd + 1) * DV)
        qh = q[:, hs]
        kc = ck_ref[0, pl.ds(hd, past, stride=N_HEADS), :].astype(BF16)
        vc = cv_ref[0, pl.ds(hd, past, stride=N_HEADS), :].astype(BF16)
        knh = kn[:, hs].astype(BF16)
        vnh = vn[:, hs].astype(BF16)
        outs = []
        for first in (True, False):
            qm = jnp.where((lane < HEAD_DIM) == first, qh, 0.0).astype(BF16)
            sf = lax.dot_general(qm, kc[:far], nt, preferred_element_type=F32)
            sc = lax.dot_general(qm, kc[far:], nt, preferred_element_type=F32) + bc_ref[hd]
            sn = lax.dot_general(qm, knh, nt, preferred_element_type=F32) + bn_ref[hd]
            m = jnp.maximum(jnp.maximum(jnp.max(sf, axis=-1, keepdims=True),
                                        jnp.max(sc, axis=-1, keepdims=True)),
                            jnp.max(sn, axis=-1, keepdims=True))
            pf = jnp.exp2(sf - m)
            pc = jnp.exp2(sc - m)
            pn = jnp.exp2(sn - m)
            l = (jnp.sum(pf, axis=-1, keepdims=True) + jnp.sum(pc, axis=-1, keepdims=True)
                 + jnp.sum(pn, axis=-1, keepdims=True))
            o = (jnp.dot(pf.astype(BF16), vc[:far], preferred_element_type=F32)
                 + jnp.dot(pc.astype(BF16), vc[far:], preferred_element_type=F32)
                 + jnp.dot(pn.astype(BF16), vnh, preferred_element_type=F32))
            outs.append(o * (1.0 / l))
        o = outs[0] - lam * outs[1]
        y = _rmsnorm(o, g_ref[...], SUBLN_EPS) * (1.0 - LAM_INIT)
        o_ref[0, :, hs] = y


def _sample_bias(rel_bias, past, s_new):
    near = SAMPLE_NEAR
    j = np.arange(s_new + near - 1)
    bias_c = _toeplitz(_bias_rows(rel_bias, j - (s_new - 1) - near), s_new, near)
    j = np.arange(2 * s_new - 1)
    bias_n = _toeplitz(_bias_rows(rel_bias, j - (s_new - 1)), s_new, s_new)
    return bias_c, bias_n


def _sample_attention(q, cache_k, cache_v, k_new, v_new, bias_c, bias_n, lam_vecs, subln_row):
    b, s, d = q.shape
    past = cache_k.shape[1] // N_HEADS
    far = past - SAMPLE_NEAR
    assert far >= 0 and SAMPLE_NEAR >= FAR_DISTANCE
    new_spec = pl.BlockSpec((1, s, d), lambda i: (i, 0, 0))
    cache_spec = pl.BlockSpec((1, past * N_HEADS, DV), lambda i: (i, 0, 0))
    return pl.pallas_call(
        functools.partial(_sample_attn_kernel, past=past, far=far),
        grid=(b,),
        in_specs=[new_spec, cache_spec, cache_spec, new_spec, new_spec,
                  _const_spec(bias_c.shape), _const_spec(bias_n.shape),
                  _const_spec((4, HEAD_DIM)), _const_spec((1, DV))],
        out_specs=new_spec,
        out_shape=jax.ShapeDtypeStruct((b, s, d), F32),
        compiler_params=pltpu.CompilerParams(
            dimension_semantics=("arbitrary",), vmem_limit_bytes=VMEM_LIMIT),
        name="sample_attention",
    )(q, cache_k, cache_v, k_new, v_new, bias_c, bias_n, lam_vecs, subln_row)


def _ffn_kernel(x_ref, a_ref, hist_ref, nm_ref, wug_ref, bg_ref, wp_ref, ps_ref, wo_ref,
                nf_ref, wgu_ref, wd_ref, nfin_ref, y_ref, pst_ref, carry_ref, ext_ref,
                *, streams, rows, pos0):
    t = pl.program_id(1)
    n = streams * rows
    d = D_MODEL

    @pl.when(t == 0)
    def _():
        carry_ref[...] = hist_ref[...]

    x = x_ref[...].reshape(n, d)
    h = _rmsnorm(x, nm_ref[...], EPS).astype(BF16)
    u = jnp.dot(h, wug_ref[:, 0:d], preferred_element_type=F32)
    zg = jnp.dot(h, wug_ref[:, d:3 * d], preferred_element_type=F32) + bg_ref[...]

    ext_ref[:, 0:HALO, :] = carry_ref[...]
    ext_ref[:, HALO:HALO + rows, :] = u.reshape(streams, rows, d)
    last = ext_ref[:, rows:rows + HALO, :]
    carry_ref[...] = last
    pst_ref[...] = last

    pos = pos0 + t * rows + lax.broadcasted_iota(jnp.int32, (1, rows, 1), 1)
    pooled = []
    for gi, w in enumerate(POOL_WINDOWS):
        cs = slice(gi * POOL_GROUP, (gi + 1) * POOL_GROUP)
        cur = ext_ref[:, HALO:HALO + rows, cs]
        s = cur
        for back in range(1, w):
            s = s + ext_ref[:, HALO - back:HALO - back + rows, cs]
        inv = 1.0 / jnp.minimum(pos + 1, w).astype(F32)
        dlt = (s * inv - cur).reshape(n, POOL_GROUP).astype(BF16)
        pooled.append(jnp.dot(dlt, wp_ref[gi], preferred_element_type=F32))
    pool = jnp.concatenate(pooled, axis=-1) * ps_ref[...]

    g_a = jax.nn.sigmoid(zg[:, 0:d])
    g_p = jax.nn.sigmoid(zg[:, d:2 * d])
    merged = g_a * a_ref[...].reshape(n, d) + g_p * pool
    x1 = x + jnp.dot(merged.astype(BF16), wo_ref[...], preferred_element_type=F32)

    h2 = _rmsnorm(x1, nf_ref[...], EPS).astype(BF16)
    gu = jnp.dot(h2, wgu_ref[...], preferred_element_type=F32)
    act = jax.nn.silu(gu[:, 0:D_FF]) * gu[:, D_FF:2 * D_FF]
    x2 = x1 + jnp.dot(act.astype(BF16), wd_ref[...], preferred_element_type=F32)
    y_ref[...] = _rmsnorm(x2, nfin_ref[...], EPS).reshape(streams, rows, d)


def _merge_ffn(x, attn, hist, weights, *, streams, rows, pos0):
    b, t, d = x.shape
    tok_spec = pl.BlockSpec((streams, rows, d), lambda i, j: (i, j, 0))
    halo_spec = pl.BlockSpec((streams, HALO, d), lambda i, j: (i, 0, 0))
    w_specs = [_const_spec(w.shape) for w in weights]
    w_specs[1] = _const_spec((d, 3 * d), index=(0, 1))
    return pl.pallas_call(
        functools.partial(_ffn_kernel, streams=streams, rows=rows, pos0=pos0),
        grid=(b // streams, t // rows),
        in_specs=[tok_spec, tok_spec, halo_spec] + w_specs,
        out_specs=[tok_spec, halo_spec],
        out_shape=[jax.ShapeDtypeStruct((b, t, d), F32),
                   jax.ShapeDtypeStruct((b, HALO, d), F32)],
        scratch_shapes=[pltpu.VMEM((streams, HALO, d), F32),
                        pltpu.VMEM((streams, HALO + rows, d), F32)],
        compiler_params=pltpu.CompilerParams(
            dimension_semantics=("arbitrary", "arbitrary"), vmem_limit_bytes=VMEM_LIMIT),
        name="merge_ffn",
    )(x, attn, hist, *weights)


def kernel(x_prompt, x_sample, cache_k, cache_v, state_pool, rel_bias, norm_mix, w_in, b_gate,
           lambda_q1, lambda_k1, lambda_q2, lambda_k2, subln_g, w_pool, pool_scale, w_out,
           norm_ffn, w_gate_up, w_down, norm_final):
    assert norm_mix.shape[0] == 1, "single layer"
    d = D_MODEL
    bp, tp, _ = x_prompt.shape
    bs, ts, _ = x_sample.shape
    past = cache_k.shape[2]
    assert tp % PROJ_ROWS == 0 and tp % FFN_ROWS == 0 and PROJ_ROWS % ATTN_BLOCK == 0
    assert ATTN_BLOCK % CHUNK == 0 and N_HEADS % ATTN_HEADS == 0 and ts >= HALO and ts % 8 == 0
    assert (past + ts - 1) // CHUNK <= past // CHUNK

    w_in_b = w_in.reshape(d, 6 * d).astype(BF16)
    ffn_weights = (norm_mix, w_in_b, b_gate, w_pool[0].astype(BF16), pool_scale,
                   w_out[0].astype(BF16), norm_ffn, w_gate_up[0].astype(BF16),
                   w_down[0].astype(BF16), norm_final.reshape(1, d))
    lam_vecs = jnp.concatenate([lambda_q1, lambda_k1, lambda_q2, lambda_k2], axis=0)

    k_p, v_p, kb, qt, vt = _project(x_prompt, norm_mix, w_in_b, rows=PROJ_ROWS, blocked=True)
    attn_p = _prompt_attention(qt, kb, vt, _prompt_bias_tiles(rel_bias), lam_vecs,
                               subln_g.reshape(DV, 1))
    y_p, pool_p = _merge_ffn(x_prompt, attn_p, jnp.zeros((bp, HALO, d), F32), ffn_weights,
                             streams=1, rows=FFN_ROWS, pos0=0)

    xs_flat = x_sample.reshape(1, bs * ts, d)
    q_s, k_s, v_s = _project(xs_flat, norm_mix, w_in_b, rows=bs * ts, blocked=False)
    q_s, k_s, v_s = (a.reshape(bs, ts, d) for a in (q_s, k_s, v_s))
    bias_c, bias_n = _sample_bias(rel_bias, past, ts)
    attn_s = _sample_attention(q_s, cache_k.reshape(bs, past * N_HEADS, DV),
                               cache_v.reshape(bs, past * N_HEADS, DV), k_s, v_s, bias_c, bias_n, lam_vecs, subln_g.reshape(1, DV))
    hist_s = jnp.pad(state_pool[0], ((0, 0), (HALO - POOL_HIST, 0), (0, 0)))
    y_s, pool_s = _merge_ffn(x_sample, attn_s, hist_s, ffn_weights, streams=bs, rows=ts, pos0=past)

    heads = (N_HEADS, DV)
    return (y_p, y_s,
            k_p.reshape(1, bp, tp, *heads), v_p.reshape(1, bp, tp, *heads),
            pool_p[None, :, HALO - POOL_HIST:, :],
            k_s.reshape(1, bs, ts, *heads), v_s.reshape(1, bs, ts, *heads),
            pool_s[None, :, HALO - POOL_HIST:, :])
```

```python
import functools
import math

import numpy as np
import jax
import jax.numpy as jnp
from jax import lax
from jax.experimental import pallas as pl
from jax.experimental.pallas import tpu as pltpu

F32 = jnp.float32
BF16 = jnp.bfloat16

D_MODEL = 1024
CHUNK = 64
HEAD_DIM = 64
DV = 2 * HEAD_DIM
DVX = DV + 16
N_HEADS = D_MODEL // DV
POOL_WINDOWS = (2, 4, 8, 16)
POOL_GROUP = D_MODEL // len(POOL_WINDOWS)
POOL_HIST = max(POOL_WINDOWS) - 1
HALO = POOL_HIST + 1
D_FF = ((8 * D_MODEL) // 3 + 255) // 256 * 256
NUM_BUCKETS = 32
MAX_DISTANCE = 128
EPS = 1e-6
SUBLN_EPS = 1e-5
LAM_INIT = 0.8 - 0.6 * math.exp(-0.3 * 0)
LOG2E = math.log2(math.e)
QK_SCALE_LOG2 = HEAD_DIM ** -0.5 * LOG2E
MASKED = -1e30

ATTN_BLOCK = 512
ATTN_HEADS = 4
PROJ_ROWS = 512
FFN_ROWS = 512
SAMPLE_NEAR = 128
VMEM_LIMIT = 56 * 1024 * 1024


def _rmsnorm(x, g, eps):
    return x * lax.rsqrt(jnp.mean(x * x, axis=-1, keepdims=True) + eps) * g


def _const_spec(shape, index=None):
    index = (0,) * len(shape) if index is None else index
    return pl.BlockSpec(shape, lambda *_: index, pipeline_mode=pl.Buffered(1))


def _proj_kernel(x_ref, g_ref, w_ref, *out_refs, rows, blocked):
    x = x_ref[0]
    h = _rmsnorm(x, g_ref[...], EPS).astype(BF16)
    q = jnp.dot(h, w_ref[:, 0:D_MODEL], preferred_element_type=F32) * QK_SCALE_LOG2
    k = jnp.dot(h, w_ref[:, D_MODEL:2 * D_MODEL], preferred_element_type=F32)
    v = jnp.dot(h, w_ref[:, 2 * D_MODEL:3 * D_MODEL], preferred_element_type=F32)
    if not blocked:
        q_ref, k_ref, v_ref = out_refs
        q_ref[0] = q
        k_ref[0] = k
        v_ref[0] = v
        return
    k_ref, v_ref, kb_ref, qt_ref, vt_ref = out_refs
    qt = q.T
    vt = v.T
    ones_rows = (lax.broadcasted_iota(jnp.int32, (DVX - DV, ATTN_BLOCK), 0) == 0).astype(BF16)
    for hd in range(N_HEADS):
        hs = slice(hd * DV, (hd + 1) * DV)
        k_ref[0, pl.ds(hd, rows, stride=N_HEADS), :] = k[:, hs]
        v_ref[0, pl.ds(hd, rows, stride=N_HEADS), :] = v[:, hs]
        for c in range(rows // ATTN_BLOCK):
            cs = slice(c * ATTN_BLOCK, (c + 1) * ATTN_BLOCK)
            kb_ref[0, hd, c] = k[cs, hs].astype(BF16)
            qt_ref[0, hd, c] = qt[hs, cs].astype(BF16)
            vt_ref[0, hd, c, 0:DV, :] = vt[hs, cs].astype(BF16)
            vt_ref[0, hd, c, DV:DVX, :] = ones_rows


def _project(x, norm_mix, w_in_b, *, rows, blocked):
    b, t, d = x.shape
    nb_step = rows // ATTN_BLOCK
    nat = jax.ShapeDtypeStruct((b, t, d), F32)
    nat_spec = pl.BlockSpec((1, rows, d), lambda i, j: (i, j, 0))
    if blocked:
        nb = t // ATTN_BLOCK
        per_head = jax.ShapeDtypeStruct((b, t * N_HEADS, DV), F32)
        per_head_spec = pl.BlockSpec((1, rows * N_HEADS, DV), lambda i, j: (i, j, 0))
        out_shape = [per_head, per_head,
                     jax.ShapeDtypeStruct((b, N_HEADS, nb, ATTN_BLOCK, DV), BF16),
                     jax.ShapeDtypeStruct((b, N_HEADS, nb, DV, ATTN_BLOCK), BF16),
                     jax.ShapeDtypeStruct((b, N_HEADS, nb, DVX, ATTN_BLOCK), BF16)]
        out_specs = [per_head_spec, per_head_spec,
                     pl.BlockSpec((1, N_HEADS, nb_step, ATTN_BLOCK, DV), lambda i, j: (i, 0, j, 0, 0)),
                     pl.BlockSpec((1, N_HEADS, nb_step, DV, ATTN_BLOCK), lambda i, j: (i, 0, j, 0, 0)),
                     pl.BlockSpec((1, N_HEADS, nb_step, DVX, ATTN_BLOCK), lambda i, j: (i, 0, j, 0, 0))]
    else:
        out_shape = [nat, nat, nat]
        out_specs = [nat_spec, nat_spec, nat_spec]
    return pl.pallas_call(
        functools.partial(_proj_kernel, rows=rows, blocked=blocked),
        grid=(b, t // rows),
        in_specs=[nat_spec, _const_spec((1, d)), _const_spec((d, 3 * d))],
        out_specs=out_specs,
        out_shape=out_shape,
        compiler_params=pltpu.CompilerParams(
            dimension_semantics=("arbitrary", "arbitrary"), vmem_limit_bytes=VMEM_LIMIT),
        name="qkv_proj_blocked" if blocked else "qkv_proj",
    )(x, norm_mix, w_in_b)


def _rel_bucket_np(rel):
    nb = NUM_BUCKETS // 2
    ret = np.where(rel > 0, nb, 0)
    n = np.abs(rel)
    max_exact = nb // 2
    large = max_exact + (np.log(np.maximum(n, 1).astype(np.float64) / max_exact)
                         / math.log(MAX_DISTANCE / max_exact) * (nb - max_exact)).astype(np.int64)
    large = np.minimum(large, nb - 1)
    return (ret + np.where(n < max_exact, n, large)).astype(np.int32)


FAR_BUCKET = int(_rel_bucket_np(np.array([-MAX_DISTANCE]))[0])
FAR_DISTANCE = int(np.min(np.nonzero(_rel_bucket_np(-np.arange(4 * MAX_DISTANCE)) == FAR_BUCKET)[0]))
assert np.all(_rel_bucket_np(-np.arange(FAR_DISTANCE, 1 << 16)) == FAR_BUCKET)


def _toeplitz(vec, rows, cols):
    h, length = vec.shape
    assert length == rows + cols - 1
    ext = jnp.concatenate([vec, jnp.zeros((h, 1), vec.dtype)], axis=1)
    skew = jnp.tile(ext, (1, rows))[:, :rows * length].reshape(h, rows, length)
    return skew[:, :, rows - 1:rows - 1 + cols]


def _bias_rows(rel_bias, rels):
    table = (rel_bias - rel_bias[FAR_BUCKET][None, :]).T * LOG2E
    return jnp.take(table, jnp.asarray(_rel_bucket_np(np.asarray(rels))), axis=1)


def _lambda_from(lv):
    s1 = jnp.sum(lv[0:1] * lv[1:2], axis=-1, keepdims=True)
    s2 = jnp.sum(lv[2:3] * lv[3:4], axis=-1, keepdims=True)
    return jnp.exp(s1) - jnp.exp(s2) + LAM_INIT


def _flash_kernel(qt_ref, kb_ref, vt_ref, bias_ref, lv_ref, g_ref, o_ref,
                  w_ref, m_ref, acc_ref, s_ref, *, blk, heads):
    i = pl.program_id(2)
    zeros = jnp.zeros((HEAD_DIM, blk), BF16)
    for hd in range(heads):
        qt = qt_ref[0, hd, 0]
        w_ref[hd, 0] = jnp.concatenate([qt[:HEAD_DIM], zeros], axis=0)
        w_ref[hd, 1] = jnp.concatenate([zeros, qt[HEAD_DIM:]], axis=0)

    chains = [(hd, mp) for hd in range(heads) for mp in range(2)]

    def scores(j, tile, hd, mp):
        s = jnp.dot(kb_ref[0, hd, j], w_ref[hd, mp], preferred_element_type=F32)
        if tile is not None:
            s = s + bias_ref[hd, tile]
        return s

    def consume(j, hd, mp, s, first):
        if first:
            m = jnp.max(s, axis=0, keepdims=True)
            m_ref[hd, mp] = m
        else:
            m = m_ref[hd, mp]
        p = jnp.exp2((s - m).astype(BF16))
        pv = jnp.dot(vt_ref[0, hd, j], p, preferred_element_type=F32)
        if first:
            acc_ref[hd, mp] = pv
        else:
            acc_ref[hd, mp] += pv

    def fast_step(j, tile, first, pending=None, lookahead=None):
        if pending is None:
            pending = scores(j, tile, *chains[0])
        out = None
        for n, (hd, mp) in enumerate(chains):
            if n + 1 < len(chains):
                nxt = scores(j, tile, *chains[n + 1])
            else:
                nxt = None
                out = lookahead() if lookahead is not None else None
            consume(j, hd, mp, pending, first)
            pending = nxt
        return out

    fast_step(i, 1, True)

    @pl.when(i >= 1)
    def _():
        fast_step(i - 1, 0, False)

    n_far = jnp.maximum(i - 1, 0)

    @pl.when(n_far > 0)
    def _():
        s_ref[...] = scores(0, None, *chains[0])

    @pl.loop(0, n_far)
    def _(j):
        ahead = lambda: scores(jnp.minimum(j + 1, n_far - 1), None, *chains[0])
        s_ref[...] = fast_step(j, None, False, pending=s_ref[...], lookahead=ahead)

    def safe_recompute():
        @pl.loop(0, 2 * heads)
        def _(c):
            hd = c // 2
            mp = c % 2
            m_ref[hd, mp] = jnp.full((1, blk), MASKED, F32)
            acc_ref[hd, mp] = jnp.zeros((DVX, blk), F32)

            @pl.loop(0, i + 1)
            def _(j):
                s = jnp.dot(kb_ref[0, hd, j], w_ref[hd, mp], preferred_element_type=F32)
                tile = jnp.clip(j - (i - 1), 0, 1)
                s = s + jnp.where(j >= i - 1, bias_ref[hd, tile], 0.0)
                m_old = m_ref[hd, mp]
                m_new = jnp.maximum(m_old, jnp.max(s, axis=0, keepdims=True))
                alpha = jnp.exp2(m_old - m_new)
                p = jnp.exp2((s - m_new).astype(BF16))
                m_ref[hd, mp] = m_new
                pv = jnp.dot(vt_ref[0, hd, j], p, preferred_element_type=F32)
                acc_ref[hd, mp] = alpha * acc_ref[hd, mp] + pv

    lam = _lambda_from(lv_ref[...])

    def finalize():
        poison = jnp.zeros((1, 1), F32)
        for hd in range(heads):
            l1 = acc_ref[hd, 0, DV:DV + 1, :]
            l2 = acc_ref[hd, 1, DV:DV + 1, :]
            o = acc_ref[hd, 0, 0:DV, :] * (1.0 / l1) - lam * (acc_ref[hd, 1, 0:DV, :] * (1.0 / l2))
            ms = jnp.mean(o * o, axis=0, keepdims=True)
            y = o * (lax.rsqrt(ms + SUBLN_EPS) * (1.0 - LAM_INIT)) * g_ref[...]
            o_ref[0, :, hd * DV:(hd + 1) * DV] = y.T
            poison = poison + jnp.sum((ms + l1 + l2) * 0.0, axis=1, keepdims=True)
        return poison

    poison = finalize()

    @pl.when(jnp.logical_not(poison[0, 0] == 0.0))
    def _():
        safe_recompute()
        finalize()


def _prompt_bias_tiles(rel_bias):
    blk = ATTN_BLOCK
    assert blk >= FAR_DISTANCE
    j = np.arange(2 * blk - 1)
    prev = _toeplitz(_bias_rows(rel_bias, -1 - j), blk, blk)
    diag = _toeplitz(_bias_rows(rel_bias, blk - 1 - j), blk, blk)
    kk = np.arange(blk)[:, None]
    qq = np.arange(blk)[None, :]
    visible = jnp.asarray((kk // CHUNK) <= (qq // CHUNK))
    return jnp.stack([prev, jnp.where(visible[None], diag, MASKED)], axis=1)


def _prompt_attention(qt, kb, vt, bias_tiles, lam_vecs, subln_col):
    b, nh, nb, _, blk = qt.shape
    t = nb * blk
    hps = ATTN_HEADS
    return pl.pallas_call(
        functools.partial(_flash_kernel, blk=blk, heads=hps),
        grid=(b, nh // hps, nb),
        in_specs=[
            pl.BlockSpec((1, hps, 1, DV, blk), lambda bi, h, i: (bi, h, i, 0, 0)),
            pl.BlockSpec((1, hps, nb, blk, DV), lambda bi, h, i: (bi, h, 0, 0, 0)),
            pl.BlockSpec((1, hps, nb, DVX, blk), lambda bi, h, i: (bi, h, 0, 0, 0)),
            pl.BlockSpec((hps, 2, blk, blk), lambda bi, h, i: (h, 0, 0, 0),
                         pipeline_mode=pl.Buffered(1)),
            _const_spec((4, HEAD_DIM)),
            _const_spec((DV, 1)),
        ],
        out_specs=pl.BlockSpec((1, blk, hps * DV), lambda bi, h, i: (bi, i, h)),
        out_shape=jax.ShapeDtypeStruct((b, t, nh * DV), F32),
        scratch_shapes=[pltpu.VMEM((hps, 2, DV, blk), BF16),
                        pltpu.VMEM((hps, 2, 1, blk), F32),
                        pltpu.VMEM((hps, 2, DVX, blk), F32),
                        pltpu.VMEM((blk, blk), F32)],
        compiler_params=pltpu.CompilerParams(
            dimension_semantics=("arbitrary", "arbitrary", "arbitrary"),
            vmem_limit_bytes=VMEM_LIMIT),
        name="prompt_attention",
    )(qt, kb, vt, bias_tiles, lam_vecs, subln_col)


def _sample_attn_kernel(q_ref, ck_ref, cv_ref, kn_ref, vn_ref, bc_ref, bn_ref, lv_ref, g_ref, o_ref,
                        *, past, far):
    lam = _lambda_from(lv_ref[...])
    q = q_ref[0]
    kn = kn_ref[0]
    vn = vn_ref[0]
    lane = lax.broadcasted_iota(jnp.int32, (q.shape[0], DV), 1)
    nt = (((1,), (1,)), ((), ()))

    for hd in range(N_HEADS):
        hs = slice(hd * DV, (hd + 1) * DV)
        qh = q[:, hs]
        kc = ck_ref[0, pl.ds(hd, past, stride=N_HEADS), :].astype(BF16)
        vc = cv_ref[0, pl.ds(hd, past, stride=N_HEADS), :].astype(BF16)
        knh = kn[:, hs].astype(BF16)
        vnh = vn[:, hs].astype(BF16)
        outs = []
        for first in (True, False):
            qm = jnp.where((lane < HEAD_DIM) == first, qh, 0.0).astype(BF16)
            sf = lax.dot_general(qm, kc[:far], nt, preferred_element_type=F32)
            sc = lax.dot_general(qm, kc[far:], nt, preferred_element_type=F32) + bc_ref[hd]
            sn = lax.dot_general(qm, knh, nt, preferred_element_type=F32) + bn_ref[hd]
            m = jnp.maximum(jnp.maximum(jnp.max(sf, axis=-1, keepdims=True),
                                        jnp.max(sc, axis=-1, keepdims=True)),
                            jnp.max(sn, axis=-1, keepdims=True))
            pf = jnp.exp2(sf - m)
            pc = jnp.exp2(sc - m)
            pn = jnp.exp2(sn - m)
            l = (jnp.sum(pf, axis=-1, keepdims=True) + jnp.sum(pc, axis=-1, keepdims=True)
                 + jnp.sum(pn, axis=-1, keepdims=True))
            o = (jnp.dot(pf.astype(BF16), vc[:far], preferred_element_type=F32)
                 + jnp.dot(pc.astype(BF16), vc[far:], preferred_element_type=F32)
                 + jnp.dot(pn.astype(BF16), vnh, preferred_element_type=F32))
            outs.append(o * (1.0 / l))
        o = outs[0] - lam * outs[1]
        y = _rmsnorm(o, g_ref[...], SUBLN_EPS) * (1.0 - LAM_INIT)
        o_ref[0, :, hs] = y


def _sample_bias(rel_bias, past, s_new):
    near = SAMPLE_NEAR
    j = np.arange(s_new + near - 1)
    bias_c = _toeplitz(_bias_rows(rel_bias, j - (s_new - 1) - near), s_new, near)
    j = np.arange(2 * s_new - 1)
    bias_n = _toeplitz(_bias_rows(rel_bias, j - (s_new - 1)), s_new, s_new)
    return bias_c, bias_n


def _sample_attention(q, cache_k, cache_v, k_new, v_new, bias_c, bias_n, lam_vecs, subln_row):
    b, s, d = q.shape
    past = cache_k.shape[1] // N_HEADS
    far = past - SAMPLE_NEAR
    assert far >= 0 and SAMPLE_NEAR >= FAR_DISTANCE
    new_spec = pl.BlockSpec((1, s, d), lambda i: (i, 0, 0))
    cache_spec = pl.BlockSpec((1, past * N_HEADS, DV), lambda i: (i, 0, 0))
    return pl.pallas_call(
        functools.partial(_sample_attn_kernel, past=past, far=far),
        grid=(b,),
        in_specs=[new_spec, cache_spec, cache_spec, new_spec, new_spec,
                  _const_spec(bias_c.shape), _const_spec(bias_n.shape),
                  _const_spec((4, HEAD_DIM)), _const_spec((1, DV))],
        out_specs=new_spec,
        out_shape=jax.ShapeDtypeStruct((b, s, d), F32),
        compiler_params=pltpu.CompilerParams(
            dimension_semantics=("arbitrary",), vmem_limit_bytes=VMEM_LIMIT),
        name="sample_attention",
    )(q, cache_k, cache_v, k_new, v_new, bias_c, bias_n, lam_vecs, subln_row)


def _ffn_kernel(x_ref, a_ref, hist_ref, nm_ref, wug_ref, bg_ref, wp_ref, ps_ref, wo_ref,
                nf_ref, wgu_ref, wd_ref, nfin_ref, y_ref, pst_ref, carry_ref, ext_ref,
                *, streams, rows, pos0):
    t = pl.program_id(1)
    n = streams * rows
    d = D_MODEL

    @pl.when(t == 0)
    def _():
        carry_ref[...] = hist_ref[...]

    x = x_ref[...].reshape(n, d)
    h = _rmsnorm(x, nm_ref[...], EPS).astype(BF16)
    u = jnp.dot(h, wug_ref[:, 0:d], preferred_element_type=F32)
    zg = jnp.dot(h, wug_ref[:, d:3 * d], preferred_element_type=F32) + bg_ref[...]

    ext_ref[:, 0:HALO, :] = carry_ref[...]
    ext_ref[:, HALO:HALO + rows, :] = u.reshape(streams, rows, d)
    last = ext_ref[:, rows:rows + HALO, :]
    carry_ref[...] = last
    pst_ref[...] = last

    pos = pos0 + t * rows + lax.broadcasted_iota(jnp.int32, (1, rows, 1), 1)
    pooled = []
    for gi, w in enumerate(POOL_WINDOWS):
        cs = slice(gi * POOL_GROUP, (gi + 1) * POOL_GROUP)
        cur = ext_ref[:, HALO:HALO + rows, cs]
        s = cur
        for back in range(1, w):
            s = s + ext_ref[:, HALO - back:HALO - back + rows, cs]
        inv = 1.0 / jnp.minimum(pos + 1, w).astype(F32)
        dlt = (s * inv - cur).reshape(n, POOL_GROUP).astype(BF16)
        pooled.append(jnp.dot(dlt, wp_ref[gi], preferred_element_type=F32))
    pool = jnp.concatenate(pooled, axis=-1) * ps_ref[...]

    g_a = jax.nn.sigmoid(zg[:, 0:d])
    g_p = jax.nn.sigmoid(zg[:, d:2 * d])
    merged = g_a * a_ref[...].reshape(n, d) + g_p * pool
    x1 = x + jnp.dot(merged.astype(BF16), wo_ref[...], preferred_element_type=F32)

    h2 = _rmsnorm(x1, nf_ref[...], EPS).astype(BF16)
    gu = jnp.dot(h2, wgu_ref[...], preferred_element_type=F32)
    act = jax.nn.silu(gu[:, 0:D_FF]) * gu[:, D_FF:2 * D_FF]
    x2 = x1 + jnp.dot(act.astype(BF16), wd_ref[...], preferred_element_type=F32)
    y_ref[...] = _rmsnorm(x2, nfin_ref[...], EPS).reshape(streams, rows, d)


def _merge_ffn(x, attn, hist, weights, *, streams, rows, pos0):
    b, t, d = x.shape
    tok_spec = pl.BlockSpec((streams, rows, d), lambda i, j: (i, j, 0))
    halo_spec = pl.BlockSpec((streams, HALO, d), lambda i, j: (i, 0, 0))
    w_specs = [_const_spec(w.shape) for w in weights]
    w_specs[1] = _const_spec((d, 3 * d), index=(0, 1))
    return pl.pallas_call(
        functools.partial(_ffn_kernel, streams=streams, rows=rows, pos0=pos0),
        grid=(b // streams, t // rows),
        in_specs=[tok_spec, tok_spec, halo_spec] + w_specs,
        out_specs=[tok_spec, halo_spec],
        out_shape=[jax.ShapeDtypeStruct((b, t, d), F32),
                   jax.ShapeDtypeStruct((b, HALO, d), F32)],
        scratch_shapes=[pltpu.VMEM((streams, HALO, d), F32),
                        pltpu.VMEM((streams, HALO + rows, d), F32)],
        compiler_params=pltpu.CompilerParams(
            dimension_semantics=("arbitrary", "arbitrary"), vmem_limit_bytes=VMEM_LIMIT),
        name="merge_ffn",
    )(x, attn, hist, *weights)


def kernel(x_prompt, x_sample, cache_k, cache_v, state_pool, rel_bias, norm_mix, w_in, b_gate,
           lambda_q1, lambda_k1, lambda_q2, lambda_k2, subln_g, w_pool, pool_scale, w_out,
           norm_ffn, w_gate_up, w_down, norm_final):
    assert norm_mix.shape[0] == 1, "single layer"
    d = D_MODEL
    bp, tp, _ = x_prompt.shape
    bs, ts, _ = x_sample.shape
    past = cache_k.shape[2]
    assert tp % PROJ_ROWS == 0 and tp % FFN_ROWS == 0 and PROJ_ROWS % ATTN_BLOCK == 0
    assert ATTN_BLOCK % CHUNK == 0 and N_HEADS % ATTN_HEADS == 0 and ts >= HALO and ts % 8 == 0
    assert (past + ts - 1) // CHUNK <= past // CHUNK

    w_in_b = w_in.reshape(d, 6 * d).astype(BF16)
    ffn_weights = (norm_mix, w_in_b, b_gate, w_pool[0].astype(BF16), pool_scale,
                   w_out[0].astype(BF16), norm_ffn, w_gate_up[0].astype(BF16),
                   w_down[0].astype(BF16), norm_final.reshape(1, d))
    lam_vecs = jnp.concatenate([lambda_q1, lambda_k1, lambda_q2, lambda_k2], axis=0)

    k_p, v_p, kb, qt, vt = _project(x_prompt, norm_mix, w_in_b, rows=PROJ_ROWS, blocked=True)
    attn_p = _prompt_attention(qt, kb, vt, _prompt_bias_tiles(rel_bias), lam_vecs,
                               subln_g.reshape(DV, 1))
    y_p, pool_p = _merge_ffn(x_prompt, attn_p, jnp.zeros((bp, HALO, d), F32), ffn_weights,
                             streams=1, rows=FFN_ROWS, pos0=0)

    xs_flat = x_sample.reshape(1, bs * ts, d)
    q_s, k_s, v_s = _project(xs_flat, norm_mix, w_in_b, rows=bs * ts, blocked=False)
    q_s, k_s, v_s = (a.reshape(bs, ts, d) for a in (q_s, k_s, v_s))
    bias_c, bias_n = _sample_bias(rel_bias, past, ts)
    attn_s = _sample_attention(q_s, cache_k.reshape(bs, past * N_HEADS, DV),
                               cache_v.reshape(bs, past * N_HEADS, DV), k_s, v_s, bias_c, bias_n, lam_vecs, subln_g.reshape(1, DV))
    hist_s = jnp.pad(state_pool[0], ((0, 0), (HALO - POOL_HIST, 0), (0, 0)))
    y_s, pool_s = _merge_ffn(x_sample, attn_s, hist_s, ffn_weights, streams=bs, rows=ts, pos0=past)

    heads = (N_HEADS, DV)
    return (y_p, y_s,
            k_p.reshape(1, bp, tp, *heads), v_p.reshape(1, bp, tp, *heads),
            pool_p[None, :, HALO - POOL_HIST:, :],
            k_s.reshape(1, bs, ts, *heads), v_s.reshape(1, bs, ts, *heads),
            pool_s[None, :, HALO - POOL_HIST:, :])
```

```python
import functools
import math

import numpy as np
import jax
import jax.numpy as jnp
from jax import lax
from jax.experimental import pallas as pl
from jax.experimental.pallas import tpu as pltpu

F32 = jnp.float32
BF16 = jnp.bfloat16

D_MODEL = 1024
CHUNK = 64
HEAD_DIM = 64
DV = 2 * HEAD_DIM
DVX = DV + 16
N_HEADS = D_MODEL // DV
POOL_WINDOWS = (2, 4, 8, 16)
POOL_GROUP = D_MODEL // len(POOL_WINDOWS)
POOL_HIST = max(POOL_WINDOWS) - 1
HALO = POOL_HIST + 1
D_FF = ((8 * D_MODEL) // 3 + 255) // 256 * 256
NUM_BUCKETS = 32
MAX_DISTANCE = 128
EPS = 1e-6
SUBLN_EPS = 1e-5
LAM_INIT = 0.8 - 0.6 * math.exp(-0.3 * 0)
LOG2E = math.log2(math.e)
QK_SCALE_LOG2 = HEAD_DIM ** -0.5 * LOG2E
MASKED = -1e30

ATTN_BLOCK = 512
ATTN_HEADS = 4
PROJ_ROWS = 1024
FFN_ROWS = 512
BIAS_CORNER = 128
SAMPLE_NEAR = 128
VMEM_LIMIT = 56 * 1024 * 1024


def _rmsnorm(x, g, eps):
    return x * lax.rsqrt(jnp.mean(x * x, axis=-1, keepdims=True) + eps) * g


def _const_spec(shape, index=None):
    index = (0,) * len(shape) if index is None else index
    return pl.BlockSpec(shape, lambda *_: index, pipeline_mode=pl.Buffered(1))


def _proj_kernel(x_ref, g_ref, w_ref, *out_refs, rows, blocked):
    x = x_ref[0]
    h = _rmsnorm(x, g_ref[...], EPS).astype(BF16)
    q = jnp.dot(h, w_ref[:, 0:D_MODEL], preferred_element_type=F32) * QK_SCALE_LOG2
    k = jnp.dot(h, w_ref[:, D_MODEL:2 * D_MODEL], preferred_element_type=F32)
    v = jnp.dot(h, w_ref[:, 2 * D_MODEL:3 * D_MODEL], preferred_element_type=F32)
    if not blocked:
        q_ref, k_ref, v_ref = out_refs
        q_ref[0] = q
        k_ref[0] = k
        v_ref[0] = v
        return
    k_ref, v_ref, kb_ref, qt_ref, vt_ref = out_refs
    qt = q.T
    vt = v.T
    ones_rows = (lax.broadcasted_iota(jnp.int32, (DVX - DV, ATTN_BLOCK), 0) == 0).astype(BF16)
    for hd in range(N_HEADS):
        hs = slice(hd * DV, (hd + 1) * DV)
        k_ref[0, pl.ds(hd, rows, stride=N_HEADS), :] = k[:, hs]
        v_ref[0, pl.ds(hd, rows, stride=N_HEADS), :] = v[:, hs]
        for c in range(rows // ATTN_BLOCK):
            cs = slice(c * ATTN_BLOCK, (c + 1) * ATTN_BLOCK)
            kb_ref[0, hd, c] = k[cs, hs].astype(BF16)
            qt_ref[0, hd, c] = qt[hs, cs].astype(BF16)
            vt_ref[0, hd, c, 0:DV, :] = vt[hs, cs].astype(BF16)
            vt_ref[0, hd, c, DV:DVX, :] = ones_rows


def _project(x, norm_mix, w_in_b, *, rows, blocked):
    b, t, d = x.shape
    nb_step = rows // ATTN_BLOCK
    nat = jax.ShapeDtypeStruct((b, t, d), F32)
    nat_spec = pl.BlockSpec((1, rows, d), lambda i, j: (i, j, 0))
    if blocked:
        nb = t // ATTN_BLOCK
        per_head = jax.ShapeDtypeStruct((b, t * N_HEADS, DV), F32)
        per_head_spec = pl.BlockSpec((1, rows * N_HEADS, DV), lambda i, j: (i, j, 0))
        out_shape = [per_head, per_head,
                     jax.ShapeDtypeStruct((b, N_HEADS, nb, ATTN_BLOCK, DV), BF16),
                     jax.ShapeDtypeStruct((b, N_HEADS, nb, DV, ATTN_BLOCK), BF16),
                     jax.ShapeDtypeStruct((b, N_HEADS, nb, DVX, ATTN_BLOCK), BF16)]
        out_specs = [per_head_spec, per_head_spec,
                     pl.BlockSpec((1, N_HEADS, nb_step, ATTN_BLOCK, DV), lambda i, j: (i, 0, j, 0, 0)),
                     pl.BlockSpec((1, N_HEADS, nb_step, DV, ATTN_BLOCK), lambda i, j: (i, 0, j, 0, 0)),
                     pl.BlockSpec((1, N_HEADS, nb_step, DVX, ATTN_BLOCK), lambda i, j: (i, 0, j, 0, 0))]
    else:
        out_shape = [nat, nat, nat]
        out_specs = [nat_spec, nat_spec, nat_spec]
    return pl.pallas_call(
        functools.partial(_proj_kernel, rows=rows, blocked=blocked),
        grid=(b, t // rows),
        in_specs=[nat_spec, _const_spec((1, d)), _const_spec((d, 3 * d))],
        out_specs=out_specs,
        out_shape=out_shape,
        compiler_params=pltpu.CompilerParams(
            dimension_semantics=("arbitrary", "arbitrary"), vmem_limit_bytes=VMEM_LIMIT),
        name="qkv_proj_blocked" if blocked else "qkv_proj",
    )(x, norm_mix, w_in_b)


def _rel_bucket_np(rel):
    nb = NUM_BUCKETS // 2
    ret = np.where(rel > 0, nb, 0)
    n = np.abs(rel)
    max_exact = nb // 2
    large = max_exact + (np.log(np.maximum(n, 1).astype(np.float64) / max_exact)
                         / math.log(MAX_DISTANCE / max_exact) * (nb - max_exact)).astype(np.int64)
    large = np.minimum(large, nb - 1)
    return (ret + np.where(n < max_exact, n, large)).astype(np.int32)


FAR_BUCKET = int(_rel_bucket_np(np.array([-MAX_DISTANCE]))[0])
FAR_DISTANCE = int(np.min(np.nonzero(_rel_bucket_np(-np.arange(4 * MAX_DISTANCE)) == FAR_BUCKET)[0]))
assert np.all(_rel_bucket_np(-np.arange(FAR_DISTANCE, 1 << 16)) == FAR_BUCKET)


def _toeplitz(vec, rows, cols):
    h, length = vec.shape
    assert length == rows + cols - 1
    ext = jnp.concatenate([vec, jnp.zeros((h, 1), vec.dtype)], axis=1)
    skew = jnp.tile(ext, (1, rows))[:, :rows * length].reshape(h, rows, length)
    return skew[:, :, rows - 1:rows - 1 + cols]


def _bias_rows(rel_bias, rels):
    table = (rel_bias - rel_bias[FAR_BUCKET][None, :]).T * LOG2E
    return jnp.take(table, jnp.asarray(_rel_bucket_np(np.asarray(rels))), axis=1)


def _lambda_from(lv):
    s1 = jnp.sum(lv[0:1] * lv[1:2], axis=-1, keepdims=True)
    s2 = jnp.sum(lv[2:3] * lv[3:4], axis=-1, keepdims=True)
    return jnp.exp(s1) - jnp.exp(s2) + LAM_INIT


def _flash_kernel(qt_ref, kb_ref, vt_ref, bdiag_ref, bcorner_ref, lv_ref, g_ref, o_ref,
                  w_ref, m_ref, acc_ref, s_ref, bias_ref, *, blk, heads):
    i = pl.program_id(2)
    half = blk // 2
    cn = BIAS_CORNER

    @pl.when(i == 0)
    def _():
        for hd in range(heads):
            corner = bcorner_ref[hd]
            diag = bdiag_ref[hd]
            bias_ref[hd, 0] = jnp.zeros((blk, blk), F32)
            bias_ref[hd, 0, blk - cn:blk, 0:cn] = corner
            bias_ref[hd, 1, 0:half, 0:half] = diag
            bias_ref[hd, 1, half:blk, half:blk] = diag
            bias_ref[hd, 1, 0:half, half:blk] = jnp.zeros((half, half), F32)
            bias_ref[hd, 1, half - cn:half, half:half + cn] = corner
            bias_ref[hd, 1, half:blk, 0:half] = jnp.full((half, half), MASKED, F32)

    zeros = jnp.zeros((HEAD_DIM, blk), BF16)
    for hd in range(heads):
        qt = qt_ref[0, hd, 0]
        w_ref[hd, 0] = jnp.concatenate([qt[:HEAD_DIM], zeros], axis=0)
        w_ref[hd, 1] = jnp.concatenate([zeros, qt[HEAD_DIM:]], axis=0)

    chains = [(hd, mp) for hd in range(heads) for mp in range(2)]

    def diagonal_step():
        def diag_scores(hd, mp):
            sa = jnp.dot(kb_ref[0, hd, i, 0:half, :], w_ref[hd, mp],
                         preferred_element_type=F32) + bias_ref[hd, 1, 0:half, :]
            sb = jnp.dot(kb_ref[0, hd, i, half:blk, :], w_ref[hd, mp, :, half:blk],
                         preferred_element_type=F32) + bias_ref[hd, 1, half:blk, half:blk]
            return sa, sb

        def diag_consume(hd, mp, sa, sb):
            m_r = jnp.maximum(jnp.max(sa[:, half:blk], axis=0, keepdims=True),
                              jnp.max(sb, axis=0, keepdims=True))
            m = jnp.concatenate([jnp.max(sa[:, 0:half], axis=0, keepdims=True), m_r], axis=1)
            m_ref[hd, mp] = m
            pa = jnp.exp2((sa - m).astype(BF16))
            pb = jnp.exp2((sb - m_r).astype(BF16))
            acc_ref[hd, mp] = jnp.dot(vt_ref[0, hd, i, :, 0:half], pa, preferred_element_type=F32)
            acc_ref[hd, mp, :, half:blk] += jnp.dot(vt_ref[0, hd, i, :, half:blk], pb,
                                                    preferred_element_type=F32)

        pending = diag_scores(*chains[0])
        for n, (hd, mp) in enumerate(chains):
            nxt = diag_scores(*chains[n + 1]) if n + 1 < len(chains) else None
            diag_consume(hd, mp, *pending)
            pending = nxt

    def scores(j, tile, hd, mp):
        s = jnp.dot(kb_ref[0, hd, j], w_ref[hd, mp], preferred_element_type=F32)
        if tile is not None:
            s = s + bias_ref[hd, tile]
        return s

    def consume(j, hd, mp, s):
        p = jnp.exp2((s - m_ref[hd, mp]).astype(BF16))
        acc_ref[hd, mp] += jnp.dot(vt_ref[0, hd, j], p, preferred_element_type=F32)

    def fast_step(j, tile, pending=None, lookahead=None):
        if pending is None:
            pending = scores(j, tile, *chains[0])
        out = None
        for n, (hd, mp) in enumerate(chains):
            if n + 1 < len(chains):
                nxt = scores(j, tile, *chains[n + 1])
            else:
                nxt = None
                out = lookahead() if lookahead is not None else None
            consume(j, hd, mp, pending)
            pending = nxt
        return out

    diagonal_step()

    @pl.when(i >= 1)
    def _():
        fast_step(i - 1, 0)

    n_far = jnp.maximum(i - 1, 0)

    def far_scores(j):
        return scores(j, None, *chains[0])

    @pl.when(n_far > 0)
    def _():
        s_ref[...] = far_scores(0)

    @pl.loop(0, n_far // 2)
    def _(jj):
        j = 2 * jj
        mid = fast_step(j, None, pending=s_ref[...], lookahead=lambda: far_scores(j + 1))
        s_ref[...] = fast_step(j + 1, None, pending=mid,
                               lookahead=lambda: far_scores(jnp.minimum(j + 2, n_far - 1)))

    @pl.when(n_far % 2 == 1)
    def _():
        fast_step(n_far - 1, None, pending=s_ref[...])

    def safe_recompute():
        @pl.loop(0, 2 * heads)
        def _(c):
            hd = c // 2
            mp = c % 2
            m_ref[hd, mp] = jnp.full((1, blk), MASKED, F32)
            acc_ref[hd, mp] = jnp.zeros((DVX, blk), F32)

            @pl.loop(0, i + 1)
            def _(j):
                s = jnp.dot(kb_ref[0, hd, j], w_ref[hd, mp], preferred_element_type=F32)
                tile = jnp.clip(j - (i - 1), 0, 1)
                s = s + jnp.where(j >= i - 1, bias_ref[hd, tile], 0.0)
                m_old = m_ref[hd, mp]
                m_new = jnp.maximum(m_old, jnp.max(s, axis=0, keepdims=True))
                alpha = jnp.exp2(m_old - m_new)
                p = jnp.exp2((s - m_new).astype(BF16))
                m_ref[hd, mp] = m_new
                pv = jnp.dot(vt_ref[0, hd, j], p, preferred_element_type=F32)
                acc_ref[hd, mp] = alpha * acc_ref[hd, mp] + pv

    lam = _lambda_from(lv_ref[...])

    def finalize():
        poison = jnp.zeros((1, 1), F32)
        for hd in range(heads):
            l1 = acc_ref[hd, 0, DV:DV + 1, :]
            l2 = acc_ref[hd, 1, DV:DV + 1, :]
            o = acc_ref[hd, 0, 0:DV, :] * (1.0 / l1) - lam * (acc_ref[hd, 1, 0:DV, :] * (1.0 / l2))
            ms = jnp.mean(o * o, axis=0, keepdims=True)
            y = o * (lax.rsqrt(ms + SUBLN_EPS) * (1.0 - LAM_INIT)) * g_ref[...]
            o_ref[0, :, hd * DV:(hd + 1) * DV] = y.T
            poison = poison + jnp.sum((ms + l1 + l2) * 0.0, axis=1, keepdims=True)
        return poison

    poison = finalize()

    @pl.when(jnp.logical_not(poison[0, 0] == 0.0))
    def _():
        safe_recompute()
        finalize()


def _prompt_bias_tiles(rel_bias):
    half = ATTN_BLOCK // 2
    cn = BIAS_CORNER
    assert cn >= FAR_DISTANCE - 1 and half >= cn and half % CHUNK == 0
    j = np.arange(2 * half - 1)
    diag = _toeplitz(_bias_rows(rel_bias, half - 1 - j), half, half)
    kk = np.arange(half)[:, None]
    qq = np.arange(half)[None, :]
    visible = jnp.asarray((kk // CHUNK) <= (qq // CHUNK))
    j = np.arange(2 * cn - 1)
    corner = _toeplitz(_bias_rows(rel_bias, -1 - j), cn, cn)
    return jnp.where(visible[None], diag, MASKED), corner


def _prompt_attention(qt, kb, vt, bias_diag, bias_corner, lam_vecs, subln_col):
    b, nh, nb, _, blk = qt.shape
    t = nb * blk
    hps = ATTN_HEADS
    return pl.pallas_call(
        functools.partial(_flash_kernel, blk=blk, heads=hps),
        grid=(b, nh // hps, nb),
        in_specs=[
            pl.BlockSpec((1, hps, 1, DV, blk), lambda bi, h, i: (bi, h, i, 0, 0)),
            pl.BlockSpec((1, hps, nb, blk, DV), lambda bi, h, i: (bi, h, 0, 0, 0)),
            pl.BlockSpec((1, hps, nb, DVX, blk), lambda bi, h, i: (bi, h, 0, 0, 0)),
            pl.BlockSpec((hps, blk // 2, blk // 2), lambda bi, h, i: (h, 0, 0),
                         pipeline_mode=pl.Buffered(1)),
            pl.BlockSpec((hps, BIAS_CORNER, BIAS_CORNER), lambda bi, h, i: (h, 0, 0),
                         pipeline_mode=pl.Buffered(1)),
            _const_spec((4, HEAD_DIM)),
            _const_spec((DV, 1)),
        ],
        out_specs=pl.BlockSpec((1, blk, hps * DV), lambda bi, h, i: (bi, i, h)),
        out_shape=jax.ShapeDtypeStruct((b, t, nh * DV), F32),
        scratch_shapes=[pltpu.VMEM((hps, 2, DV, blk), BF16),
                        pltpu.VMEM((hps, 2, 1, blk), F32),
                        pltpu.VMEM((hps, 2, DVX, blk), F32),
                        pltpu.VMEM((blk, blk), F32),
                        pltpu.VMEM((hps, 2, blk, blk), F32)],
        compiler_params=pltpu.CompilerParams(
            dimension_semantics=("arbitrary", "arbitrary", "arbitrary"),
            vmem_limit_bytes=VMEM_LIMIT),
        name="prompt_attention",
    )(qt, kb, vt, bias_diag, bias_corner, lam_vecs, subln_col)


def _sample_attn_kernel(q_ref, ck_ref, cv_ref, kn_ref, vn_ref, bc_ref, bn_ref, lv_ref, g_ref, o_ref,
                        *, past, far):
    lam = _lambda_from(lv_ref[...])
    q = q_ref[0]
    kn = kn_ref[0]
    vn = vn_ref[0]
    lane = lax.broadcasted_iota(jnp.int32, (q.shape[0], DV), 1)
    nt = (((1,), (1,)), ((), ()))

    for hd in range(N_HEADS):
        hs = slice(hd * DV, (hd + 1) * DV)
        qh = q[:, hs]
        kc = ck_ref[0, pl.ds(hd, past, stride=N_HEADS), :].astype(BF16)
        vc = cv_ref[0, pl.ds(hd, past, stride=N_HEADS), :].astype(BF16)
        knh = kn[:, hs].astype(BF16)
        vnh = vn[:, hs].astype(BF16)
        outs = []
        for first in (True, False):
            qm = jnp.where((lane < HEAD_DIM) == first, qh, 0.0).astype(BF16)
            sf = lax.dot_general(qm, kc[:far], nt, preferred_element_type=F32)
            sc = lax.dot_general(qm, kc[far:], nt, preferred_element_type=F32) + bc_ref[hd]
            sn = lax.dot_general(qm, knh, nt, preferred_element_type=F32) + bn_ref[hd]
            m = jnp.maximum(jnp.maximum(jnp.max(sf, axis=-1, keepdims=True),
                                        jnp.max(sc, axis=-1, keepdims=True)),
                            jnp.max(sn, axis=-1, keepdims=True))
            pf = jnp.exp2(sf - m)
            pc = jnp.exp2(sc - m)
            pn = jnp.exp2(sn - m)
            l = (jnp.sum(pf, axis=-1, keepdims=True) + jnp.sum(pc, axis=-1, keepdims=True)
                 + jnp.sum(pn, axis=-1, keepdims=True))
            o = (jnp.dot(pf.astype(BF16), vc[:far], preferred_element_type=F32)
                 + jnp.dot(pc.astype(BF16), vc[far:], preferred_element_type=F32)
                 + jnp.dot(pn.astype(BF16), vnh, preferred_element_type=F32))
            outs.append(o * (1.0 / l))
        o = outs[0] - lam * outs[1]
        y = _rmsnorm(o, g_ref[...], SUBLN_EPS) * (1.0 - LAM_INIT)
        o_ref[0, :, hs] = y


def _sample_bias(rel_bias, past, s_new):
    near = SAMPLE_NEAR
    j = np.arange(s_new + near - 1)
    bias_c = _toeplitz(_bias_rows(rel_bias, j - (s_new - 1) - near), s_new, near)
    j = np.arange(2 * s_new - 1)
    bias_n = _toeplitz(_bias_rows(rel_bias, j - (s_new - 1)), s_new, s_new)
    return bias_c, bias_n


def _sample_attention(q, cache_k, cache_v, k_new, v_new, bias_c, bias_n, lam_vecs, subln_row):
    b, s, d = q.shape
    past = cache_k.shape[1] // N_HEADS
    far = past - SAMPLE_NEAR
    assert far >= 0 and SAMPLE_NEAR >= FAR_DISTANCE
    new_spec = pl.BlockSpec((1, s, d), lambda i: (i, 0, 0))
    cache_spec = pl.BlockSpec((1, past * N_HEADS, DV), lambda i: (i, 0, 0))
    return pl.pallas_call(
        functools.partial(_sample_attn_kernel, past=past, far=far),
        grid=(b,),
        in_specs=[new_spec, cache_spec, cache_spec, new_spec, new_spec,
                  _const_spec(bias_c.shape), _const_spec(bias_n.shape),
                  _const_spec((4, HEAD_DIM)), _const_spec((1, DV))],
        out_specs=new_spec,
        out_shape=jax.ShapeDtypeStruct((b, s, d), F32),
        compiler_params=pltpu.CompilerParams(
            dimension_semantics=("arbitrary",), vmem_limit_bytes=VMEM_LIMIT),
        name="sample_attention",
    )(q, cache_k, cache_v, k_new, v_new, bias_c, bias_n, lam_vecs, subln_row)


def _ffn_kernel(x_ref, a_ref, hist_ref, nm_ref, wug_ref, bg_ref, wp_ref, ps_ref, wo_ref,
                nf_ref, wgu_ref, wd_ref, nfin_ref, y_ref, pst_ref, carry_ref, ext_ref,
                *, streams, rows, pos0):
    t = pl.program_id(1)
    n = streams * rows
    d = D_MODEL

    @pl.when(t == 0)
    def _():
        carry_ref[...] = hist_ref[...]

    x = x_ref[...].reshape(n, d)
    h = _rmsnorm(x, nm_ref[...], EPS).astype(BF16)
    u = jnp.dot(h, wug_ref[:, 0:d], preferred_element_type=F32)
    zg = jnp.dot(h, wug_ref[:, d:3 * d], preferred_element_type=F32) + bg_ref[...]

    ext_ref[:, 0:HALO, :] = carry_ref[...]
    ext_ref[:, HALO:HALO + rows, :] = u.reshape(streams, rows, d)
    last = ext_ref[:, rows:rows + HALO, :]
    carry_ref[...] = last
    pst_ref[...] = last

    pos = pos0 + t * rows + lax.broadcasted_iota(jnp.int32, (1, rows, 1), 1)
    pooled = []
    for gi, w in enumerate(POOL_WINDOWS):
        cs = slice(gi * POOL_GROUP, (gi + 1) * POOL_GROUP)
        cur = ext_ref[:, HALO:HALO + rows, cs]
        s = cur
        for back in range(1, w):
            s = s + ext_ref[:, HALO - back:HALO - back + rows, cs]
        inv = 1.0 / jnp.minimum(pos + 1, w).astype(F32)
        dlt = (s * inv - cur).reshape(n, POOL_GROUP).astype(BF16)
        pooled.append(jnp.dot(dlt, wp_ref[gi], preferred_element_type=F32))
    pool = jnp.concatenate(pooled, axis=-1) * ps_ref[...]

    g_a = jax.nn.sigmoid(zg[:, 0:d])
    g_p = jax.nn.sigmoid(zg[:, d:2 * d])
    merged = g_a * a_ref[...].reshape(n, d) + g_p * pool
    x1 = x + jnp.dot(merged.astype(BF16), wo_ref[...], preferred_element_type=F32)

    h2 = _rmsnorm(x1, nf_ref[...], EPS).astype(BF16)
    gu = jnp.dot(h2, wgu_ref[...], preferred_element_type=F32)
    act = jax.nn.silu(gu[:, 0:D_FF]) * gu[:, D_FF:2 * D_FF]
    x2 = x1 + jnp.dot(act.astype(BF16), wd_ref[...], preferred_element_type=F32)
    y_ref[...] = _rmsnorm(x2, nfin_ref[...], EPS).reshape(streams, rows, d)


def _merge_ffn(x, attn, hist, weights, *, streams, rows, pos0):
    b, t, d = x.shape
    tok_spec = pl.BlockSpec((streams, rows, d), lambda i, j: (i, j, 0))
    halo_spec = pl.BlockSpec((streams, HALO, d), lambda i, j: (i, 0, 0))
    w_specs = [_const_spec(w.shape) for w in weights]
    w_specs[1] = _const_spec((d, 3 * d), index=(0, 1))
    return pl.pallas_call(
        functools.partial(_ffn_kernel, streams=streams, rows=rows, pos0=pos0),
        grid=(b // streams, t // rows),
        in_specs=[tok_spec, tok_spec, halo_spec] + w_specs,
        out_specs=[tok_spec, halo_spec],
        out_shape=[jax.ShapeDtypeStruct((b, t, d), F32),
                   jax.ShapeDtypeStruct((b, HALO, d), F32)],
        scratch_shapes=[pltpu.VMEM((streams, HALO, d), F32),
                        pltpu.VMEM((streams, HALO + rows, d), F32)],
        compiler_params=pltpu.CompilerParams(
            dimension_semantics=("arbitrary", "arbitrary"), vmem_limit_bytes=VMEM_LIMIT),
        name="merge_ffn",
    )(x, attn, hist, *weights)


def kernel(x_prompt, x_sample, cache_k, cache_v, state_pool, rel_bias, norm_mix, w_in, b_gate,
           lambda_q1, lambda_k1, lambda_q2, lambda_k2, subln_g, w_pool, pool_scale, w_out,
           norm_ffn, w_gate_up, w_down, norm_final):
    assert norm_mix.shape[0] == 1, "single layer"
    d = D_MODEL
    bp, tp, _ = x_prompt.shape
    bs, ts, _ = x_sample.shape
    past = cache_k.shape[2]
    assert tp % PROJ_ROWS == 0 and tp % FFN_ROWS == 0 and PROJ_ROWS % ATTN_BLOCK == 0
    assert ATTN_BLOCK % CHUNK == 0 and N_HEADS % ATTN_HEADS == 0 and ts >= HALO and ts % 8 == 0
    assert (past + ts - 1) // CHUNK <= past // CHUNK

    w_in_b = w_in.reshape(d, 6 * d).astype(BF16)
    ffn_weights = (norm_mix, w_in_b, b_gate, w_pool[0].astype(BF16), pool_scale,
                   w_out[0].astype(BF16), norm_ffn, w_gate_up[0].astype(BF16),
                   w_down[0].astype(BF16), norm_final.reshape(1, d))
    lam_vecs = jnp.concatenate([lambda_q1, lambda_k1, lambda_q2, lambda_k2], axis=0)

    k_p, v_p, kb, qt, vt = _project(x_prompt, norm_mix, w_in_b, rows=PROJ_ROWS, blocked=True)
    attn_p = _prompt_attention(qt, kb, vt, *_prompt_bias_tiles(rel_bias), lam_vecs,
                               subln_g.reshape(DV, 1))
    y_p, pool_p = _merge_ffn(x_prompt, attn_p, jnp.zeros((bp, HALO, d), F32), ffn_weights,
                             streams=1, rows=FFN_ROWS, pos0=0)

    xs_flat = x_sample.reshape(1, bs * ts, d)
    q_s, k_s, v_s = _project(xs_flat, norm_mix, w_in_b, rows=bs * ts, blocked=False)
    q_s, k_s, v_s = (a.reshape(bs, ts, d) for a in (q_s, k_s, v_s))
    bias_c, bias_n = _sample_bias(rel_bias, past, ts)
    attn_s = _sample_attention(q_s, cache_k.reshape(bs, past * N_HEADS, DV),
                               cache_v.reshape(bs, past * N_HEADS, DV), k_s, v_s, bias_c, bias_n, lam_vecs, subln_g.reshape(1, DV))
    hist_s = jnp.pad(state_pool[0], ((0, 0), (HALO - POOL_HIST, 0), (0, 0)))
    y_s, pool_s = _merge_ffn(x_sample, attn_s, hist_s, ffn_weights, streams=bs, rows=ts, pos0=past)

    heads = (N_HEADS, DV)
    return (y_p, y_s,
            k_p.reshape(1, bp, tp, *heads), v_p.reshape(1, bp, tp, *heads),
            pool_p[None, :, HALO - POOL_HIST:, :],
            k_s.reshape(1, bs, ts, *heads), v_s.reshape(1, bs, ts, *heads),
            pool_s[None, :, HALO - POOL_HIST:, :])
```

```python
import functools
import math

import numpy as np
import jax
import jax.numpy as jnp
from jax import lax
from jax.experimental import pallas as pl
from jax.experimental.pallas import tpu as pltpu

F32 = jnp.float32
BF16 = jnp.bfloat16

D_MODEL = 1024
CHUNK = 64
HEAD_DIM = 64
DV = 2 * HEAD_DIM
N_HEADS = D_MODEL // DV
POOL_WINDOWS = (2, 4, 8, 16)
POOL_GROUP = D_MODEL // len(POOL_WINDOWS)
POOL_HIST = max(POOL_WINDOWS) - 1
HALO = POOL_HIST + 1
D_FF = ((8 * D_MODEL) // 3 + 255) // 256 * 256
NUM_BUCKETS = 32
MAX_DISTANCE = 128
EPS = 1e-6
SUBLN_EPS = 1e-5
LAM_INIT = 0.8 - 0.6 * math.exp(-0.3 * 0)
LOG2E = math.log2(math.e)
QK_SCALE_LOG2 = HEAD_DIM ** -0.5 * LOG2E
MASKED = -1e30

ATTN_BLOCK = 512
ATTN_HEADS = 4
PROJ_ROWS = 1024
FFN_ROWS = 512
BIAS_CORNER = 128
SAMPLE_NEAR = 128
VMEM_LIMIT = 56 * 1024 * 1024


def _rmsnorm(x, g, eps):
    return x * lax.rsqrt(jnp.mean(x * x, axis=-1, keepdims=True) + eps) * g


def _const_spec(shape, index=None):
    index = (0,) * len(shape) if index is None else index
    return pl.BlockSpec(shape, lambda *_: index, pipeline_mode=pl.Buffered(1))


def _proj_kernel(x_ref, g_ref, w_ref, *out_refs, rows, blocked):
    x = x_ref[0]
    h = _rmsnorm(x, g_ref[...], EPS).astype(BF16)
    q = jnp.dot(h, w_ref[:, 0:D_MODEL], preferred_element_type=F32) * QK_SCALE_LOG2
    k = jnp.dot(h, w_ref[:, D_MODEL:2 * D_MODEL], preferred_element_type=F32)
    v = jnp.dot(h, w_ref[:, 2 * D_MODEL:3 * D_MODEL], preferred_element_type=F32)
    if not blocked:
        q_ref, k_ref, v_ref = out_refs
        q_ref[0] = q
        k_ref[0] = k
        v_ref[0] = v
        return
    k_ref, v_ref, kb_ref, qt_ref, vt_ref = out_refs
    qt = q.T
    vt = v.T
    for hd in range(N_HEADS):
        hs = slice(hd * DV, (hd + 1) * DV)
        k_ref[0, pl.ds(hd, rows, stride=N_HEADS), :] = k[:, hs]
        v_ref[0, pl.ds(hd, rows, stride=N_HEADS), :] = v[:, hs]
        for c in range(rows // ATTN_BLOCK):
            cs = slice(c * ATTN_BLOCK, (c + 1) * ATTN_BLOCK)
            kb_ref[0, hd, c] = k[cs, hs].astype(BF16)
            qt_ref[0, hd, c] = qt[hs, cs].astype(BF16)
            vt_ref[0, hd, c] = vt[hs, cs].astype(BF16)


def _project(x, norm_mix, w_in_b, *, rows, blocked):
    b, t, d = x.shape
    nb_step = rows // ATTN_BLOCK
    nat = jax.ShapeDtypeStruct((b, t, d), F32)
    nat_spec = pl.BlockSpec((1, rows, d), lambda i, j: (i, j, 0))
    if blocked:
        nb = t // ATTN_BLOCK
        per_head = jax.ShapeDtypeStruct((b, t * N_HEADS, DV), F32)
        per_head_spec = pl.BlockSpec((1, rows * N_HEADS, DV), lambda i, j: (i, j, 0))
        out_shape = [per_head, per_head,
                     jax.ShapeDtypeStruct((b, N_HEADS, nb, ATTN_BLOCK, DV), BF16),
                     jax.ShapeDtypeStruct((b, N_HEADS, nb, DV, ATTN_BLOCK), BF16),
                     jax.ShapeDtypeStruct((b, N_HEADS, nb, DV, ATTN_BLOCK), BF16)]
        out_specs = [per_head_spec, per_head_spec,
                     pl.BlockSpec((1, N_HEADS, nb_step, ATTN_BLOCK, DV), lambda i, j: (i, 0, j, 0, 0)),
                     pl.BlockSpec((1, N_HEADS, nb_step, DV, ATTN_BLOCK), lambda i, j: (i, 0, j, 0, 0)),
                     pl.BlockSpec((1, N_HEADS, nb_step, DV, ATTN_BLOCK), lambda i, j: (i, 0, j, 0, 0))]
    else:
        out_shape = [nat, nat, nat]
        out_specs = [nat_spec, nat_spec, nat_spec]
    return pl.pallas_call(
        functools.partial(_proj_kernel, rows=rows, blocked=blocked),
        grid=(b, t // rows),
        in_specs=[nat_spec, _const_spec((1, d)), _const_spec((d, 3 * d))],
        out_specs=out_specs,
        out_shape=out_shape,
        compiler_params=pltpu.CompilerParams(
            dimension_semantics=("arbitrary", "arbitrary"), vmem_limit_bytes=VMEM_LIMIT),
        name="qkv_proj_blocked" if blocked else "qkv_proj",
    )(x, norm_mix, w_in_b)


def _rel_bucket_np(rel):
    nb = NUM_BUCKETS // 2
    ret = np.where(rel > 0, nb, 0)
    n = np.abs(rel)
    max_exact = nb // 2
    large = max_exact + (np.log(np.maximum(n, 1).astype(np.float64) / max_exact)
                         / math.log(MAX_DISTANCE / max_exact) * (nb - max_exact)).astype(np.int64)
    large = np.minimum(large, nb - 1)
    return (ret + np.where(n < max_exact, n, large)).astype(np.int32)


FAR_BUCKET = int(_rel_bucket_np(np.array([-MAX_DISTANCE]))[0])
FAR_DISTANCE = int(np.min(np.nonzero(_rel_bucket_np(-np.arange(4 * MAX_DISTANCE)) == FAR_BUCKET)[0]))
assert np.all(_rel_bucket_np(-np.arange(FAR_DISTANCE, 1 << 16)) == FAR_BUCKET)


def _toeplitz(vec, rows, cols):
    h, length = vec.shape
    assert length == rows + cols - 1
    ext = jnp.concatenate([vec, jnp.zeros((h, 1), vec.dtype)], axis=1)
    skew = jnp.tile(ext, (1, rows))[:, :rows * length].reshape(h, rows, length)
    return skew[:, :, rows - 1:rows - 1 + cols]


def _bias_rows(rel_bias, rels):
    table = (rel_bias - rel_bias[FAR_BUCKET][None, :]).T * LOG2E
    return jnp.take(table, jnp.asarray(_rel_bucket_np(np.asarray(rels))), axis=1)


def _lambda_from(lv):
    s1 = jnp.sum(lv[0:1] * lv[1:2], axis=-1, keepdims=True)
    s2 = jnp.sum(lv[2:3] * lv[3:4], axis=-1, keepdims=True)
    return jnp.exp(s1) - jnp.exp(s2) + LAM_INIT


def _flash_kernel(qt_ref, kb_ref, vt_ref, bdiag_ref, bcorner_ref, lv_ref, g_ref, o_ref,
                  w_ref, m_ref, acc_ref, l_ref, s_ref, bias_ref, *, blk, heads):
    i = pl.program_id(2)
    half = blk // 2
    cn = BIAS_CORNER

    @pl.when(i == 0)
    def _():
        for hd in range(heads):
            corner = bcorner_ref[hd]
            diag = bdiag_ref[hd]
            bias_ref[hd, 0] = jnp.zeros((blk, blk), F32)
            bias_ref[hd, 0, blk - cn:blk, 0:cn] = corner
            bias_ref[hd, 1, 0:half, 0:half] = diag
            bias_ref[hd, 1, half:blk, half:blk] = diag
            bias_ref[hd, 1, 0:half, half:blk] = jnp.zeros((half, half), F32)
            bias_ref[hd, 1, half - cn:half, half:half + cn] = corner
            bias_ref[hd, 1, half:blk, 0:half] = jnp.full((half, half), MASKED, F32)

    zeros = jnp.zeros((HEAD_DIM, blk), BF16)
    for hd in range(heads):
        qt = qt_ref[0, hd, 0]
        w_ref[hd, 0] = jnp.concatenate([qt[:HEAD_DIM], zeros], axis=0)
        w_ref[hd, 1] = jnp.concatenate([zeros, qt[HEAD_DIM:]], axis=0)

    chains = [(hd, mp) for hd in range(heads) for mp in range(2)]

    def key_sums(p):
        return jnp.sum(p.astype(F32).reshape(p.shape[0] // 8, 8, p.shape[1]), axis=0)

    def diagonal_step():
        def diag_scores(hd, mp):
            sa = jnp.dot(kb_ref[0, hd, i, 0:half, :], w_ref[hd, mp],
                         preferred_element_type=F32) + bias_ref[hd, 1, 0:half, :]
            sb = jnp.dot(kb_ref[0, hd, i, half:blk, :], w_ref[hd, mp, :, half:blk],
                         preferred_element_type=F32) + bias_ref[hd, 1, half:blk, half:blk]
            return sa, sb

        def diag_consume(hd, mp, sa, sb):
            m_r = jnp.maximum(jnp.max(sa[:, half:blk], axis=0, keepdims=True),
                              jnp.max(sb, axis=0, keepdims=True))
            m = jnp.concatenate([jnp.max(sa[:, 0:half], axis=0, keepdims=True), m_r], axis=1)
            m_ref[hd, mp] = m
            pa = jnp.exp2((sa - m).astype(BF16))
            pb = jnp.exp2((sb - m_r).astype(BF16))
            acc_ref[hd, mp] = jnp.dot(vt_ref[0, hd, i, :, 0:half], pa, preferred_element_type=F32)
            acc_ref[hd, mp, :, half:blk] += jnp.dot(vt_ref[0, hd, i, :, half:blk], pb,
                                                    preferred_element_type=F32)
            l_ref[hd, mp] = key_sums(pa)
            l_ref[hd, mp, :, half:blk] += key_sums(pb)

        pending = diag_scores(*chains[0])
        for n, (hd, mp) in enumerate(chains):
            nxt = diag_scores(*chains[n + 1]) if n + 1 < len(chains) else None
            diag_consume(hd, mp, *pending)
            pending = nxt

    def scores(j, tile, hd, mp):
        s = jnp.dot(kb_ref[0, hd, j], w_ref[hd, mp], preferred_element_type=F32)
        if tile is not None:
            s = s + bias_ref[hd, tile]
        return s

    def consume(j, hd, mp, s):
        p = jnp.exp2((s - m_ref[hd, mp]).astype(BF16))
        acc_ref[hd, mp] += jnp.dot(vt_ref[0, hd, j], p, preferred_element_type=F32)
        l_ref[hd, mp] += key_sums(p)

    def fast_step(j, tile, pending=None, lookahead=None):
        if pending is None:
            pending = scores(j, tile, *chains[0])
        out = None
        for n, (hd, mp) in enumerate(chains):
            if n + 1 < len(chains):
                nxt = scores(j, tile, *chains[n + 1])
            else:
                nxt = None
                out = lookahead() if lookahead is not None else None
            consume(j, hd, mp, pending)
            pending = nxt
        return out

    diagonal_step()

    @pl.when(i >= 1)
    def _():
        fast_step(i - 1, 0)

    n_far = jnp.maximum(i - 1, 0)

    def far_scores(j):
        return scores(j, None, *chains[0])

    @pl.when(n_far > 0)
    def _():
        s_ref[...] = far_scores(0)

    @pl.loop(0, n_far // 2)
    def _(jj):
        j = 2 * jj
        mid = fast_step(j, None, pending=s_ref[...], lookahead=lambda: far_scores(j + 1))
        s_ref[...] = fast_step(j + 1, None, pending=mid,
                               lookahead=lambda: far_scores(jnp.minimum(j + 2, n_far - 1)))

    @pl.when(n_far % 2 == 1)
    def _():
        fast_step(n_far - 1, None, pending=s_ref[...])

    def safe_recompute():
        @pl.loop(0, 2 * heads)
        def _(c):
            hd = c // 2
            mp = c % 2
            m_ref[hd, mp] = jnp.full((1, blk), MASKED, F32)
            acc_ref[hd, mp] = jnp.zeros((DV, blk), F32)
            l_ref[hd, mp] = jnp.zeros((8, blk), F32)

            @pl.loop(0, i + 1)
            def _(j):
                s = jnp.dot(kb_ref[0, hd, j], w_ref[hd, mp], preferred_element_type=F32)
                tile = jnp.clip(j - (i - 1), 0, 1)
                s = s + jnp.where(j >= i - 1, bias_ref[hd, tile], 0.0)
                m_old = m_ref[hd, mp]
                m_new = jnp.maximum(m_old, jnp.max(s, axis=0, keepdims=True))
                alpha = jnp.exp2(m_old - m_new)
                p = jnp.exp2((s - m_new).astype(BF16))
                m_ref[hd, mp] = m_new
                pv = jnp.dot(vt_ref[0, hd, j], p, preferred_element_type=F32)
                acc_ref[hd, mp] = alpha * acc_ref[hd, mp] + pv
                l_ref[hd, mp] = alpha * l_ref[hd, mp] + key_sums(p)

    lam = _lambda_from(lv_ref[...])

    def finalize():
        poison = jnp.zeros((1, 1), F32)
        for hd in range(heads):
            l1 = jnp.sum(l_ref[hd, 0], axis=0, keepdims=True)
            l2 = jnp.sum(l_ref[hd, 1], axis=0, keepdims=True)
            o = acc_ref[hd, 0] * (1.0 / l1) - lam * (acc_ref[hd, 1] * (1.0 / l2))
            ms = jnp.mean(o * o, axis=0, keepdims=True)
            y = o * (lax.rsqrt(ms + SUBLN_EPS) * (1.0 - LAM_INIT)) * g_ref[...]
            o_ref[0, :, hd * DV:(hd + 1) * DV] = y.T
            poison = poison + jnp.sum((ms + l1 + l2) * 0.0, axis=1, keepdims=True)
        return poison

    poison = finalize()

    @pl.when(jnp.logical_not(poison[0, 0] == 0.0))
    def _():
        safe_recompute()
        finalize()


def _prompt_bias_tiles(rel_bias):
    half = ATTN_BLOCK // 2
    cn = BIAS_CORNER
    assert cn >= FAR_DISTANCE - 1 and half >= cn and half % CHUNK == 0
    j = np.arange(2 * half - 1)
    diag = _toeplitz(_bias_rows(rel_bias, half - 1 - j), half, half)
    kk = np.arange(half)[:, None]
    qq = np.arange(half)[None, :]
    visible = jnp.asarray((kk // CHUNK) <= (qq // CHUNK))
    j = np.arange(2 * cn - 1)
    corner = _toeplitz(_bias_rows(rel_bias, -1 - j), cn, cn)
    return jnp.where(visible[None], diag, MASKED), corner


def _prompt_attention(qt, kb, vt, bias_diag, bias_corner, lam_vecs, subln_col):
    b, nh, nb, _, blk = qt.shape
    t = nb * blk
    hps = ATTN_HEADS
    return pl.pallas_call(
        functools.partial(_flash_kernel, blk=blk, heads=hps),
        grid=(b, nh // hps, nb),
        in_specs=[
            pl.BlockSpec((1, hps, 1, DV, blk), lambda bi, h, i: (bi, h, i, 0, 0)),
            pl.BlockSpec((1, hps, nb, blk, DV), lambda bi, h, i: (bi, h, 0, 0, 0)),
            pl.BlockSpec((1, hps, nb, DV, blk), lambda bi, h, i: (bi, h, 0, 0, 0)),
            pl.BlockSpec((hps, blk // 2, blk // 2), lambda bi, h, i: (h, 0, 0),
                         pipeline_mode=pl.Buffered(1)),
            pl.BlockSpec((hps, BIAS_CORNER, BIAS_CORNER), lambda bi, h, i: (h, 0, 0),
                         pipeline_mode=pl.Buffered(1)),
            _const_spec((4, HEAD_DIM)),
            _const_spec((DV, 1)),
        ],
        out_specs=pl.BlockSpec((1, blk, hps * DV), lambda bi, h, i: (bi, i, h)),
        out_shape=jax.ShapeDtypeStruct((b, t, nh * DV), F32),
        scratch_shapes=[pltpu.VMEM((hps, 2, DV, blk), BF16),
                        pltpu.VMEM((hps, 2, 1, blk), F32),
                        pltpu.VMEM((hps, 2, DV, blk), F32),
                        pltpu.VMEM((hps, 2, 8, blk), F32),
                        pltpu.VMEM((blk, blk), F32),
                        pltpu.VMEM((hps, 2, blk, blk), F32)],
        compiler_params=pltpu.CompilerParams(
            dimension_semantics=("arbitrary", "arbitrary", "arbitrary"),
            vmem_limit_bytes=VMEM_LIMIT),
        name="prompt_attention",
    )(qt, kb, vt, bias_diag, bias_corner, lam_vecs, subln_col)


def _sample_attn_kernel(q_ref, ck_ref, cv_ref, kn_ref, vn_ref, bc_ref, bn_ref, lv_ref, g_ref, o_ref,
                        *, past, far):
    lam = _lambda_from(lv_ref[...])
    q = q_ref[0]
    kn = kn_ref[0]
    vn = vn_ref[0]
    lane = lax.broadcasted_iota(jnp.int32, (q.shape[0], DV), 1)
    nt = (((1,), (1,)), ((), ()))

    for hd in range(N_HEADS):
        hs = slice(hd * DV, (hd + 1) * DV)
        qh = q[:, hs]
        kc = ck_ref[0, pl.ds(hd, past, stride=N_HEADS), :].astype(BF16)
        vc = cv_ref[0, pl.ds(hd, past, stride=N_HEADS), :].astype(BF16)
        knh = kn[:, hs].astype(BF16)
        vnh = vn[:, hs].astype(BF16)
        outs = []
        for first in (True, False):
            qm = jnp.where((lane < HEAD_DIM) == first, qh, 0.0).astype(BF16)
            sf = lax.dot_general(qm, kc[:far], nt, preferred_element_type=F32)
            sc = lax.dot_general(qm, kc[far:], nt, preferred_element_type=F32) + bc_ref[hd]
            sn = lax.dot_general(qm, knh, nt, preferred_element_type=F32) + bn_ref[hd]
            m = jnp.maximum(jnp.maximum(jnp.max(sf, axis=-1, keepdims=True),
                                        jnp.max(sc, axis=-1, keepdims=True)),
                            jnp.max(sn, axis=-1, keepdims=True))
            pf = jnp.exp2(sf - m)
            pc = jnp.exp2(sc - m)
            pn = jnp.exp2(sn - m)
            l = (jnp.sum(pf, axis=-1, keepdims=True) + jnp.sum(pc, axis=-1, keepdims=True)
                 + jnp.sum(pn, axis=-1, keepdims=True))
            o = (jnp.dot(pf.astype(BF16), vc[:far], preferred_element_type=F32)
                 + jnp.dot(pc.astype(BF16), vc[far:], preferred_element_type=F32)
                 + jnp.dot(pn.astype(BF16), vnh, preferred_element_type=F32))
            outs.append(o * (1.0 / l))
        o = outs[0] - lam * outs[1]
        y = _rmsnorm(o, g_ref[...], SUBLN_EPS) * (1.0 - LAM_INIT)
        o_ref[0, :, hs] = y


def _sample_bias(rel_bias, past, s_new):
    near = SAMPLE_NEAR
    j = np.arange(s_new + near - 1)
    bias_c = _toeplitz(_bias_rows(rel_bias, j - (s_new - 1) - near), s_new, near)
    j = np.arange(2 * s_new - 1)
    bias_n = _toeplitz(_bias_rows(rel_bias, j - (s_new - 1)), s_new, s_new)
    return bias_c, bias_n


def _sample_attention(q, cache_k, cache_v, k_new, v_new, bias_c, bias_n, lam_vecs, subln_row):
    b, s, d = q.shape
    past = cache_k.shape[1] // N_HEADS
    far = past - SAMPLE_NEAR
    assert far >= 0 and SAMPLE_NEAR >= FAR_DISTANCE
    new_spec = pl.BlockSpec((1, s, d), lambda i: (i, 0, 0))
    cache_spec = pl.BlockSpec((1, past * N_HEADS, DV), lambda i: (i, 0, 0))
    return pl.pallas_call(
        functools.partial(_sample_attn_kernel, past=past, far=far),
        grid=(b,),
        in_specs=[new_spec, cache_spec, cache_spec, new_spec, new_spec,
                  _const_spec(bias_c.shape), _const_spec(bias_n.shape),
                  _const_spec((4, HEAD_DIM)), _const_spec((1, DV))],
        out_specs=new_spec,
        out_shape=jax.ShapeDtypeStruct((b, s, d), F32),
        compiler_params=pltpu.CompilerParams(
            dimension_semantics=("arbitrary",), vmem_limit_bytes=VMEM_LIMIT),
        name="sample_attention",
    )(q, cache_k, cache_v, k_new, v_new, bias_c, bias_n, lam_vecs, subln_row)


def _ffn_kernel(x_ref, a_ref, hist_ref, nm_ref, wug_ref, bg_ref, wp_ref, ps_ref, wo_ref,
                nf_ref, wgu_ref, wd_ref, nfin_ref, y_ref, pst_ref, carry_ref, ext_ref,
                *, streams, rows, pos0):
    t = pl.program_id(1)
    n = streams * rows
    d = D_MODEL

    @pl.when(t == 0)
    def _():
        carry_ref[...] = hist_ref[...]

    x = x_ref[...].reshape(n, d)
    h = _rmsnorm(x, nm_ref[...], EPS).astype(BF16)
    u = jnp.dot(h, wug_ref[:, 0:d], preferred_element_type=F32)
    zg = jnp.dot(h, wug_ref[:, d:3 * d], preferred_element_type=F32) + bg_ref[...]

    ext_ref[:, 0:HALO, :] = carry_ref[...]
    ext_ref[:, HALO:HALO + rows, :] = u.reshape(streams, rows, d)
    last = ext_ref[:, rows:rows + HALO, :]
    carry_ref[...] = last
    pst_ref[...] = last

    pos = pos0 + t * rows + lax.broadcasted_iota(jnp.int32, (1, rows, 1), 1)
    pooled = []
    for gi, w in enumerate(POOL_WINDOWS):
        cs = slice(gi * POOL_GROUP, (gi + 1) * POOL_GROUP)
        cur = ext_ref[:, HALO:HALO + rows, cs]
        s = cur
        for back in range(1, w):
            s = s + ext_ref[:, HALO - back:HALO - back + rows, cs]
        inv = 1.0 / jnp.minimum(pos + 1, w).astype(F32)
        dlt = (s * inv - cur).reshape(n, POOL_GROUP).astype(BF16)
        pooled.append(jnp.dot(dlt, wp_ref[gi], preferred_element_type=F32))
    pool = jnp.concatenate(pooled, axis=-1) * ps_ref[...]

    g_a = jax.nn.sigmoid(zg[:, 0:d])
    g_p = jax.nn.sigmoid(zg[:, d:2 * d])
    merged = g_a * a_ref[...].reshape(n, d) + g_p * pool
    x1 = x + jnp.dot(merged.astype(BF16), wo_ref[...], preferred_element_type=F32)

    h2 = _rmsnorm(x1, nf_ref[...], EPS).astype(BF16)
    gu = jnp.dot(h2, wgu_ref[...], preferred_element_type=F32)
    act = jax.nn.silu(gu[:, 0:D_FF]) * gu[:, D_FF:2 * D_FF]
    x2 = x1 + jnp.dot(act.astype(BF16), wd_ref[...], preferred_element_type=F32)
    y_ref[...] = _rmsnorm(x2, nfin_ref[...], EPS).reshape(streams, rows, d)


def _merge_ffn(x, attn, hist, weights, *, streams, rows, pos0):
    b, t, d = x.shape
    tok_spec = pl.BlockSpec((streams, rows, d), lambda i, j: (i, j, 0))
    halo_spec = pl.BlockSpec((streams, HALO, d), lambda i, j: (i, 0, 0))
    w_specs = [_const_spec(w.shape) for w in weights]
    w_specs[1] = _const_spec((d, 3 * d), index=(0, 1))
    return pl.pallas_call(
        functools.partial(_ffn_kernel, streams=streams, rows=rows, pos0=pos0),
        grid=(b // streams, t // rows),
        in_specs=[tok_spec, tok_spec, halo_spec] + w_specs,
        out_specs=[tok_spec, halo_spec],
        out_shape=[jax.ShapeDtypeStruct((b, t, d), F32),
                   jax.ShapeDtypeStruct((b, HALO, d), F32)],
        scratch_shapes=[pltpu.VMEM((streams, HALO, d), F32),
                        pltpu.VMEM((streams, HALO + rows, d), F32)],
        compiler_params=pltpu.CompilerParams(
            dimension_semantics=("arbitrary", "arbitrary"), vmem_limit_bytes=VMEM_LIMIT),
        name="merge_ffn",
    )(x, attn, hist, *weights)


def kernel(x_prompt, x_sample, cache_k, cache_v, state_pool, rel_bias, norm_mix, w_in, b_gate,
           lambda_q1, lambda_k1, lambda_q2, lambda_k2, subln_g, w_pool, pool_scale, w_out,
           norm_ffn, w_gate_up, w_down, norm_final):
    assert norm_mix.shape[0] == 1, "single layer"
    d = D_MODEL
    bp, tp, _ = x_prompt.shape
    bs, ts, _ = x_sample.shape
    past = cache_k.shape[2]
    assert tp % PROJ_ROWS == 0 and tp % FFN_ROWS == 0 and PROJ_ROWS % ATTN_BLOCK == 0
    assert ATTN_BLOCK % CHUNK == 0 and N_HEADS % ATTN_HEADS == 0 and ts >= HALO and ts % 8 == 0
    assert (past + ts - 1) // CHUNK <= past // CHUNK

    w_in_b = w_in.reshape(d, 6 * d).astype(BF16)
    ffn_weights = (norm_mix, w_in_b, b_gate, w_pool[0].astype(BF16), pool_scale,
                   w_out[0].astype(BF16), norm_ffn, w_gate_up[0].astype(BF16),
                   w_down[0].astype(BF16), norm_final.reshape(1, d))
    lam_vecs = jnp.concatenate([lambda_q1, lambda_k1, lambda_q2, lambda_k2], axis=0)

    k_p, v_p, kb, qt, vt = _project(x_prompt, norm_mix, w_in_b, rows=PROJ_ROWS, blocked=True)
    attn_p = _prompt_attention(qt, kb, vt, *_prompt_bias_tiles(rel_bias), lam_vecs,
                               subln_g.reshape(DV, 1))
    y_p, pool_p = _merge_ffn(x_prompt, attn_p, jnp.zeros((bp, HALO, d), F32), ffn_weights,
                             streams=1, rows=FFN_ROWS, pos0=0)

    xs_flat = x_sample.reshape(1, bs * ts, d)
    q_s, k_s, v_s = _project(xs_flat, norm_mix, w_in_b, rows=bs * ts, blocked=False)
    q_s, k_s, v_s = (a.reshape(bs, ts, d) for a in (q_s, k_s, v_s))
    bias_c, bias_n = _sample_bias(rel_bias, past, ts)
    attn_s = _sample_attention(q_s, cache_k.reshape(bs, past * N_HEADS, DV),
                               cache_v.reshape(bs, past * N_HEADS, DV), k_s, v_s, bias_c, bias_n, lam_vecs, subln_g.reshape(1, DV))
    hist_s = jnp.pad(state_pool[0], ((0, 0), (HALO - POOL_HIST, 0), (0, 0)))
    y_s, pool_s = _merge_ffn(x_sample, attn_s, hist_s, ffn_weights, streams=bs, rows=ts, pos0=past)

    heads = (N_HEADS, DV)
    return (y_p, y_s,
            k_p.reshape(1, bp, tp, *heads), v_p.reshape(1, bp, tp, *heads),
            pool_p[None, :, HALO - POOL_HIST:, :],
            k_s.reshape(1, bs, ts, *heads), v_s.reshape(1, bs, ts, *heads),
            pool_s[None, :, HALO - POOL_HIST:, :])
```

```python
import functools
import math

import numpy as np
import jax
import jax.numpy as jnp
from jax import lax
from jax.experimental import pallas as pl
from jax.experimental.pallas import tpu as pltpu

F32 = jnp.float32
BF16 = jnp.bfloat16

D_MODEL = 1024
CHUNK = 64
HEAD_DIM = 64
DV = 2 * HEAD_DIM
N_HEADS = D_MODEL // DV
POOL_WINDOWS = (2, 4, 8, 16)
POOL_GROUP = D_MODEL // len(POOL_WINDOWS)
POOL_HIST = max(POOL_WINDOWS) - 1
HALO = POOL_HIST + 1
D_FF = ((8 * D_MODEL) // 3 + 255) // 256 * 256
NUM_BUCKETS = 32
MAX_DISTANCE = 128
EPS = 1e-6
SUBLN_EPS = 1e-5
LAM_INIT = 0.8 - 0.6 * math.exp(-0.3 * 0)
LOG2E = math.log2(math.e)
QK_SCALE_LOG2 = HEAD_DIM ** -0.5 * LOG2E
MASKED = -1e30

ATTN_BLOCK = 512
ATTN_HEADS = 4
PROJ_ROWS = 1024
FFN_ROWS = 512
BIAS_CORNER = 128
SAMPLE_NEAR = 128
VMEM_LIMIT = 56 * 1024 * 1024
SUBLANES = 8


def _rmsnorm(x, g, eps):
    return x * lax.rsqrt(jnp.mean(x * x, axis=-1, keepdims=True) + eps) * g


def _const_spec(shape, index=None):
    index = (0,) * len(shape) if index is None else index
    return pl.BlockSpec(shape, lambda *_: index, pipeline_mode=pl.Buffered(1))


def _proj_kernel(x_ref, g_ref, w_ref, *out_refs, rows, blocked):
    x = x_ref[0]
    h = _rmsnorm(x, g_ref[...], EPS).astype(BF16)
    q = jnp.dot(h, w_ref[:, 0:D_MODEL], preferred_element_type=F32) * QK_SCALE_LOG2
    k = jnp.dot(h, w_ref[:, D_MODEL:2 * D_MODEL], preferred_element_type=F32)
    v = jnp.dot(h, w_ref[:, 2 * D_MODEL:3 * D_MODEL], preferred_element_type=F32)
    if not blocked:
        q_ref, k_ref, v_ref = out_refs
        q_ref[0] = q
        k_ref[0] = k
        v_ref[0] = v
        return
    k_ref, v_ref, kb_ref, qt_ref, vt_ref = out_refs
    qt = q.T
    vt = v.T
    for hd in range(N_HEADS):
        hs = slice(hd * DV, (hd + 1) * DV)
        k_ref[0, pl.ds(hd, rows, stride=N_HEADS), :] = k[:, hs]
        v_ref[0, pl.ds(hd, rows, stride=N_HEADS), :] = v[:, hs]
        for c in range(rows // ATTN_BLOCK):
            cs = slice(c * ATTN_BLOCK, (c + 1) * ATTN_BLOCK)
            kb_ref[0, hd, c] = k[cs, hs].astype(BF16)
            qt_ref[0, hd, c] = qt[hs, cs].astype(BF16)
            vt_ref[0, hd, c] = vt[hs, cs].astype(BF16)


def _project(x, norm_mix, w_in_b, *, rows, blocked):
    b, t, d = x.shape
    nb_step = rows // ATTN_BLOCK
    nat = jax.ShapeDtypeStruct((b, t, d), F32)
    nat_spec = pl.BlockSpec((1, rows, d), lambda i, j: (i, j, 0))
    if blocked:
        nb = t // ATTN_BLOCK
        per_head = jax.ShapeDtypeStruct((b, t * N_HEADS, DV), F32)
        per_head_spec = pl.BlockSpec((1, rows * N_HEADS, DV), lambda i, j: (i, j, 0))
        out_shape = [per_head, per_head,
                     jax.ShapeDtypeStruct((b, N_HEADS, nb, ATTN_BLOCK, DV), BF16),
                     jax.ShapeDtypeStruct((b, N_HEADS, nb, DV, ATTN_BLOCK), BF16),
                     jax.ShapeDtypeStruct((b, N_HEADS, nb, DV, ATTN_BLOCK), BF16)]
        out_specs = [per_head_spec, per_head_spec,
                     pl.BlockSpec((1, N_HEADS, nb_step, ATTN_BLOCK, DV), lambda i, j: (i, 0, j, 0, 0)),
                     pl.BlockSpec((1, N_HEADS, nb_step, DV, ATTN_BLOCK), lambda i, j: (i, 0, j, 0, 0)),
                     pl.BlockSpec((1, N_HEADS, nb_step, DV, ATTN_BLOCK), lambda i, j: (i, 0, j, 0, 0))]
    else:
        out_shape = [nat, nat, nat]
        out_specs = [nat_spec, nat_spec, nat_spec]
    return pl.pallas_call(
        functools.partial(_proj_kernel, rows=rows, blocked=blocked),
        grid=(b, t // rows),
        in_specs=[nat_spec, _const_spec((1, d)), _const_spec((d, 3 * d))],
        out_specs=out_specs,
        out_shape=out_shape,
        compiler_params=pltpu.CompilerParams(
            dimension_semantics=("arbitrary", "arbitrary"), vmem_limit_bytes=VMEM_LIMIT),
        name="qkv_proj_blocked" if blocked else "qkv_proj",
    )(x, norm_mix, w_in_b)


def _rel_bucket_np(rel):
    nb = NUM_BUCKETS // 2
    ret = np.where(rel > 0, nb, 0)
    n = np.abs(rel)
    max_exact = nb // 2
    large = max_exact + (np.log(np.maximum(n, 1).astype(np.float64) / max_exact)
                         / math.log(MAX_DISTANCE / max_exact) * (nb - max_exact)).astype(np.int64)
    large = np.minimum(large, nb - 1)
    return (ret + np.where(n < max_exact, n, large)).astype(np.int32)


FAR_BUCKET = int(_rel_bucket_np(np.array([-MAX_DISTANCE]))[0])
FAR_DISTANCE = int(np.min(np.nonzero(_rel_bucket_np(-np.arange(4 * MAX_DISTANCE)) == FAR_BUCKET)[0]))
assert np.all(_rel_bucket_np(-np.arange(FAR_DISTANCE, 1 << 16)) == FAR_BUCKET)


def _toeplitz(vec, rows, cols):
    h, length = vec.shape
    assert length == rows + cols - 1
    ext = jnp.concatenate([vec, jnp.zeros((h, 1), vec.dtype)], axis=1)
    skew = jnp.tile(ext, (1, rows))[:, :rows * length].reshape(h, rows, length)
    return skew[:, :, rows - 1:rows - 1 + cols]


def _bias_rows(rel_bias, rels):
    table = (rel_bias - rel_bias[FAR_BUCKET][None, :]).T * LOG2E
    return jnp.take(table, jnp.asarray(_rel_bucket_np(np.asarray(rels))), axis=1)


def _lambda_from(lv):
    s1 = jnp.sum(lv[0:1] * lv[1:2], axis=-1, keepdims=True)
    s2 = jnp.sum(lv[2:3] * lv[3:4], axis=-1, keepdims=True)
    return jnp.exp(s1) - jnp.exp(s2) + LAM_INIT


def _flash_kernel(qt_ref, kb_ref, vt_ref, bdiag_ref, bcorner_ref, lv_ref, g_ref, o_ref,
                  w_ref, m_ref, acc_ref, l_ref, s_ref, bias_ref, *, blk, heads):
    i = pl.program_id(2)
    half = blk // 2
    cn = BIAS_CORNER

    @pl.when(i == 0)
    def _():
        for hd in range(heads):
            corner = bcorner_ref[hd]
            diag = bdiag_ref[hd]
            bias_ref[hd, 0] = jnp.zeros((blk, blk), F32)
            bias_ref[hd, 0, blk - cn:blk, 0:cn] = corner
            bias_ref[hd, 1, 0:half, 0:half] = diag
            bias_ref[hd, 1, half:blk, half:blk] = diag
            bias_ref[hd, 1, 0:half, half:blk] = jnp.zeros((half, half), F32)
            bias_ref[hd, 1, half - cn:half, half:half + cn] = corner
            bias_ref[hd, 1, half:blk, 0:half] = jnp.full((half, half), MASKED, F32)

    zeros = jnp.zeros((HEAD_DIM, blk), BF16)
    for hd in range(heads):
        qt = qt_ref[0, hd, 0]
        w_ref[hd, 0] = jnp.concatenate([qt[:HEAD_DIM], zeros], axis=0)
        w_ref[hd, 1] = jnp.concatenate([zeros, qt[HEAD_DIM:]], axis=0)

    chains = [(hd, mp) for hd in range(heads) for mp in range(2)]

    def key_sums(p):
        return jnp.sum(p.astype(F32).reshape(p.shape[0] // SUBLANES, SUBLANES, p.shape[1]), axis=0)

    def diagonal_step(lookahead):
        def diag_scores(hd, mp):
            sa = jnp.dot(kb_ref[0, hd, i, 0:half, :], w_ref[hd, mp],
                         preferred_element_type=F32) + bias_ref[hd, 1, 0:half, :]
            sb = jnp.dot(kb_ref[0, hd, i, half:blk, :], w_ref[hd, mp, :, half:blk],
                         preferred_element_type=F32) + bias_ref[hd, 1, half:blk, half:blk]
            return sa, sb

        def diag_consume(hd, mp, sa, sb):
            m_r = jnp.maximum(jnp.max(sa[:, half:blk], axis=0, keepdims=True),
                              jnp.max(sb, axis=0, keepdims=True))
            m = jnp.concatenate([jnp.max(sa[:, 0:half], axis=0, keepdims=True), m_r], axis=1)
            m_ref[hd, mp] = m
            pa = jnp.exp2((sa - m).astype(BF16))
            pb = jnp.exp2((sb - m_r).astype(BF16))
            acc_ref[hd, mp] = jnp.dot(vt_ref[0, hd, i, :, 0:half], pa, preferred_element_type=F32)
            acc_ref[hd, mp, :, half:blk] += jnp.dot(vt_ref[0, hd, i, :, half:blk], pb,
                                                    preferred_element_type=F32)
            l_ref[hd, mp] = key_sums(pa)
            l_ref[hd, mp, :, half:blk] += key_sums(pb)

        pending = diag_scores(*chains[0])
        out = None
        for n, (hd, mp) in enumerate(chains):
            if n + 1 < len(chains):
                nxt = diag_scores(*chains[n + 1])
            else:
                nxt = None
                out = lookahead()
            diag_consume(hd, mp, *pending)
            pending = nxt
        return out

    def scores(j, tile, hd, mp):
        s = jnp.dot(kb_ref[0, hd, j], w_ref[hd, mp], preferred_element_type=F32)
        if tile is not None:
            s = s + bias_ref[hd, tile]
        return s

    def consume(j, hd, mp, s):
        p = jnp.exp2((s - m_ref[hd, mp]).astype(BF16))
        acc_ref[hd, mp] += jnp.dot(vt_ref[0, hd, j], p, preferred_element_type=F32)
        l_ref[hd, mp] += key_sums(p)

    def fast_step(j, tile, pending=None, lookahead=None):
        if pending is None:
            pending = scores(j, tile, *chains[0])
        out = None
        for n, (hd, mp) in enumerate(chains):
            if n + 1 < len(chains):
                nxt = scores(j, tile, *chains[n + 1])
            else:
                nxt = None
                out = lookahead() if lookahead is not None else None
            consume(j, hd, mp, pending)
            pending = nxt
        return out

    n_far = jnp.maximum(i - 1, 0)

    def far_scores(j):
        return scores(j, None, *chains[0])

    s_ref[...] = diagonal_step(lambda: scores(jnp.maximum(i - 1, 0), 0, *chains[0]))

    @pl.when(i >= 1)
    def _():
        s_ref[...] = fast_step(i - 1, 0, pending=s_ref[...], lookahead=lambda: far_scores(0))

    @pl.loop(0, n_far // 2)
    def _(jj):
        j = 2 * jj
        mid = fast_step(j, None, pending=s_ref[...], lookahead=lambda: far_scores(j + 1))
        s_ref[...] = fast_step(j + 1, None, pending=mid,
                               lookahead=lambda: far_scores(jnp.minimum(j + 2, n_far - 1)))

    @pl.when(n_far % 2 == 1)
    def _():
        fast_step(n_far - 1, None, pending=s_ref[...])

    def safe_recompute():
        @pl.loop(0, 2 * heads)
        def _(c):
            hd = c // 2
            mp = c % 2
            m_ref[hd, mp] = jnp.full((1, blk), MASKED, F32)
            acc_ref[hd, mp] = jnp.zeros((DV, blk), F32)
            l_ref[hd, mp] = jnp.zeros((SUBLANES, blk), F32)

            @pl.loop(0, i + 1)
            def _(j):
                s = jnp.dot(kb_ref[0, hd, j], w_ref[hd, mp], preferred_element_type=F32)
                tile = jnp.clip(j - (i - 1), 0, 1)
                s = s + jnp.where(j >= i - 1, bias_ref[hd, tile], 0.0)
                m_old = m_ref[hd, mp]
                m_new = jnp.maximum(m_old, jnp.max(s, axis=0, keepdims=True))
                alpha = jnp.exp2(m_old - m_new)
                p = jnp.exp2((s - m_new).astype(BF16))
                m_ref[hd, mp] = m_new
                pv = jnp.dot(vt_ref[0, hd, j], p, preferred_element_type=F32)
                acc_ref[hd, mp] = alpha * acc_ref[hd, mp] + pv
                l_ref[hd, mp] = alpha * l_ref[hd, mp] + key_sums(p)

    lam = _lambda_from(lv_ref[...])

    def finalize():
        poison = jnp.zeros((1, 1), F32)
        for hd in range(heads):
            l1 = jnp.sum(l_ref[hd, 0], axis=0, keepdims=True)
            l2 = jnp.sum(l_ref[hd, 1], axis=0, keepdims=True)
            o = acc_ref[hd, 0] * (1.0 / l1) - lam * (acc_ref[hd, 1] * (1.0 / l2))
            ms = jnp.mean(o * o, axis=0, keepdims=True)
            y = o * (lax.rsqrt(ms + SUBLN_EPS) * (1.0 - LAM_INIT)) * g_ref[...]
            o_ref[0, :, hd * DV:(hd + 1) * DV] = y.T
            poison = poison + jnp.sum((ms + l1 + l2) * 0.0, axis=1, keepdims=True)
        return poison

    poison = finalize()

    @pl.when(jnp.logical_not(poison[0, 0] == 0.0))
    def _():
        safe_recompute()
        finalize()


def _prompt_bias_tiles(rel_bias):
    half = ATTN_BLOCK // 2
    cn = BIAS_CORNER
    assert cn >= FAR_DISTANCE - 1 and half >= cn and half % CHUNK == 0
    j = np.arange(2 * half - 1)
    diag = _toeplitz(_bias_rows(rel_bias, half - 1 - j), half, half)
    kk = np.arange(half)[:, None]
    qq = np.arange(half)[None, :]
    visible = jnp.asarray((kk // CHUNK) <= (qq // CHUNK))
    j = np.arange(2 * cn - 1)
    corner = _toeplitz(_bias_rows(rel_bias, -1 - j), cn, cn)
    return jnp.where(visible[None], diag, MASKED), corner


def _prompt_attention(qt, kb, vt, bias_diag, bias_corner, lam_vecs, subln_col):
    b, nh, nb, _, blk = qt.shape
    t = nb * blk
    hps = ATTN_HEADS
    return pl.pallas_call(
        functools.partial(_flash_kernel, blk=blk, heads=hps),
        grid=(b, nh // hps, nb),
        in_specs=[
            pl.BlockSpec((1, hps, 1, DV, blk), lambda bi, h, i: (bi, h, i, 0, 0)),
            pl.BlockSpec((1, hps, nb, blk, DV), lambda bi, h, i: (bi, h, 0, 0, 0)),
            pl.BlockSpec((1, hps, nb, DV, blk), lambda bi, h, i: (bi, h, 0, 0, 0)),
            pl.BlockSpec((hps, blk // 2, blk // 2), lambda bi, h, i: (h, 0, 0),
                         pipeline_mode=pl.Buffered(1)),
            pl.BlockSpec((hps, BIAS_CORNER, BIAS_CORNER), lambda bi, h, i: (h, 0, 0),
                         pipeline_mode=pl.Buffered(1)),
            _const_spec((4, HEAD_DIM)),
            _const_spec((DV, 1)),
        ],
        out_specs=pl.BlockSpec((1, blk, hps * DV), lambda bi, h, i: (bi, i, h)),
        out_shape=jax.ShapeDtypeStruct((b, t, nh * DV), F32),
        scratch_shapes=[pltpu.VMEM((hps, 2, DV, blk), BF16),
                        pltpu.VMEM((hps, 2, 1, blk), F32),
                        pltpu.VMEM((hps, 2, DV, blk), F32),
                        pltpu.VMEM((hps, 2, SUBLANES, blk), F32),
                        pltpu.VMEM((blk, blk), F32),
                        pltpu.VMEM((hps, 2, blk, blk), F32)],
        compiler_params=pltpu.CompilerParams(
            dimension_semantics=("arbitrary", "arbitrary", "arbitrary"),
            vmem_limit_bytes=VMEM_LIMIT),
        name="prompt_attention",
    )(qt, kb, vt, bias_diag, bias_corner, lam_vecs, subln_col)


def _sample_attn_kernel(q_ref, ck_ref, cv_ref, kn_ref, vn_ref, bc_ref, bn_ref, lv_ref, g_ref, o_ref,
                        *, past, far):
    lam = _lambda_from(lv_ref[...])
    q = q_ref[0]
    kn = kn_ref[0]
    vn = vn_ref[0]
    s_new = q.shape[0]
    lane = lax.broadcasted_iota(jnp.int32, (s_new, DV), 1)
    nt = (((1,), (1,)), ((), ()))

    def head_scores(hd):
        hs = slice(hd * DV, (hd + 1) * DV)
        qh = q[:, hs]
        qq = jnp.concatenate([jnp.where(lane < HEAD_DIM, qh, 0.0),
                              jnp.where(lane >= HEAD_DIM, qh, 0.0)], axis=0).astype(BF16)
        kc = ck_ref[0, pl.ds(hd, past, stride=N_HEADS), :].astype(BF16)
        sf = lax.dot_general(qq, kc[:far], nt, preferred_element_type=F32)
        sc = lax.dot_general(qq, kc[far:], nt, preferred_element_type=F32) + bc_ref[hd]
        sn = lax.dot_general(qq, kn[:, hs].astype(BF16), nt, preferred_element_type=F32) + bn_ref[hd]
        return sf, sc, sn

    def head_output(hd, sf, sc, sn):
        hs = slice(hd * DV, (hd + 1) * DV)
        vc = cv_ref[0, pl.ds(hd, past, stride=N_HEADS), :].astype(BF16)
        m = jnp.maximum(jnp.maximum(jnp.max(sf, axis=-1, keepdims=True),
                                    jnp.max(sc, axis=-1, keepdims=True)),
                        jnp.max(sn, axis=-1, keepdims=True))
        pf = jnp.exp2(sf - m)
        pc = jnp.exp2(sc - m)
        pn = jnp.exp2(sn - m)
        l = (jnp.sum(pf, axis=-1, keepdims=True) + jnp.sum(pc, axis=-1, keepdims=True)
             + jnp.sum(pn, axis=-1, keepdims=True))
        o = (jnp.dot(pf.astype(BF16), vc[:far], preferred_element_type=F32)
             + jnp.dot(pc.astype(BF16), vc[far:], preferred_element_type=F32)
             + jnp.dot(pn.astype(BF16), vn[:, hs].astype(BF16), preferred_element_type=F32))
        o = o * (1.0 / l)
        o = o[0:s_new] - lam * o[s_new:2 * s_new]
        o_ref[0, :, hs] = _rmsnorm(o, g_ref[...], SUBLN_EPS) * (1.0 - LAM_INIT)

    pending = head_scores(0)
    for hd in range(N_HEADS):
        nxt = head_scores(hd + 1) if hd + 1 < N_HEADS else None
        head_output(hd, *pending)
        pending = nxt


def _sample_bias(rel_bias, past, s_new):
    near = SAMPLE_NEAR
    j = np.arange(s_new + near - 1)
    bias_c = _toeplitz(_bias_rows(rel_bias, j - (s_new - 1) - near), s_new, near)
    j = np.arange(2 * s_new - 1)
    bias_n = _toeplitz(_bias_rows(rel_bias, j - (s_new - 1)), s_new, s_new)
    twice = lambda t: jnp.concatenate([t, t], axis=1)
    return twice(bias_c), twice(bias_n)


def _sample_attention(q, cache_k, cache_v, k_new, v_new, bias_c, bias_n, lam_vecs, subln_row):
    b, s, d = q.shape
    past = cache_k.shape[1] // N_HEADS
    far = past - SAMPLE_NEAR
    assert far >= 0 and SAMPLE_NEAR >= FAR_DISTANCE
    new_spec = pl.BlockSpec((1, s, d), lambda i: (i, 0, 0))
    cache_spec = pl.BlockSpec((1, past * N_HEADS, DV), lambda i: (i, 0, 0))
    return pl.pallas_call(
        functools.partial(_sample_attn_kernel, past=past, far=far),
        grid=(b,),
        in_specs=[new_spec, cache_spec, cache_spec, new_spec, new_spec,
                  _const_spec(bias_c.shape), _const_spec(bias_n.shape),
                  _const_spec((4, HEAD_DIM)), _const_spec((1, DV))],
        out_specs=new_spec,
        out_shape=jax.ShapeDtypeStruct((b, s, d), F32),
        compiler_params=pltpu.CompilerParams(
            dimension_semantics=("arbitrary",), vmem_limit_bytes=VMEM_LIMIT),
        name="sample_attention",
    )(q, cache_k, cache_v, k_new, v_new, bias_c, bias_n, lam_vecs, subln_row)


def _ffn_kernel(x_ref, a_ref, hist_ref, nm_ref, wug_ref, bg_ref, wp_ref, ps_ref, wo_ref,
                nf_ref, wgu_ref, wd_ref, nfin_ref, y_ref, pst_ref, carry_ref, ext_ref,
                *, streams, rows, pos0):
    t = pl.program_id(1)
    n = streams * rows
    d = D_MODEL

    @pl.when(t == 0)
    def _():
        carry_ref[...] = hist_ref[...]

    x = x_ref[...].reshape(n, d)
    h = _rmsnorm(x, nm_ref[...], EPS).astype(BF16)
    u = jnp.dot(h, wug_ref[:, 0:d], preferred_element_type=F32)
    zg = jnp.dot(h, wug_ref[:, d:3 * d], preferred_element_type=F32) + bg_ref[...]

    ext_ref[:, 0:HALO, :] = carry_ref[...]
    ext_ref[:, HALO:HALO + rows, :] = u.reshape(streams, rows, d)
    last = ext_ref[:, rows:rows + HALO, :]
    carry_ref[...] = last
    pst_ref[...] = last

    pos = pos0 + t * rows + lax.broadcasted_iota(jnp.int32, (1, rows, 1), 1)
    pooled = []
    for gi, w in enumerate(POOL_WINDOWS):
        cs = slice(gi * POOL_GROUP, (gi + 1) * POOL_GROUP)
        cur = ext_ref[:, HALO:HALO + rows, cs]
        s = cur
        for back in range(1, w):
            s = s + ext_ref[:, HALO - back:HALO - back + rows, cs]
        inv = 1.0 / jnp.minimum(pos + 1, w).astype(F32)
        dlt = (s * inv - cur).reshape(n, POOL_GROUP).astype(BF16)
        pooled.append(jnp.dot(dlt, wp_ref[gi], preferred_element_type=F32))
    pool = jnp.concatenate(pooled, axis=-1) * ps_ref[...]

    g_a = jax.nn.sigmoid(zg[:, 0:d])
    g_p = jax.nn.sigmoid(zg[:, d:2 * d])
    merged = g_a * a_ref[...].reshape(n, d) + g_p * pool
    x1 = x + jnp.dot(merged.astype(BF16), wo_ref[...], preferred_element_type=F32)

    h2 = _rmsnorm(x1, nf_ref[...], EPS).astype(BF16)
    gu = jnp.dot(h2, wgu_ref[...], preferred_element_type=F32)
    act = jax.nn.silu(gu[:, 0:D_FF]) * gu[:, D_FF:2 * D_FF]
    x2 = x1 + jnp.dot(act.astype(BF16), wd_ref[...], preferred_element_type=F32)
    y_ref[...] = _rmsnorm(x2, nfin_ref[...], EPS).reshape(streams, rows, d)


def _merge_ffn(x, attn, hist, weights, *, streams, rows, pos0):
    b, t, d = x.shape
    tok_spec = pl.BlockSpec((streams, rows, d), lambda i, j: (i, j, 0))
    halo_spec = pl.BlockSpec((streams, HALO, d), lambda i, j: (i, 0, 0))
    w_specs = [_const_spec(w.shape) for w in weights]
    w_specs[1] = _const_spec((d, 3 * d), index=(0, 1))
    return pl.pallas_call(
        functools.partial(_ffn_kernel, streams=streams, rows=rows, pos0=pos0),
        grid=(b // streams, t // rows),
        in_specs=[tok_spec, tok_spec, halo_spec] + w_specs,
        out_specs=[tok_spec, halo_spec],
        out_shape=[jax.ShapeDtypeStruct((b, t, d), F32),
                   jax.ShapeDtypeStruct((b, HALO, d), F32)],
        scratch_shapes=[pltpu.VMEM((streams, HALO, d), F32),
                        pltpu.VMEM((streams, HALO + rows, d), F32)],
        compiler_params=pltpu.CompilerParams(
            dimension_semantics=("arbitrary", "arbitrary"), vmem_limit_bytes=VMEM_LIMIT),
        name="merge_ffn",
    )(x, attn, hist, *weights)


def kernel(x_prompt, x_sample, cache_k, cache_v, state_pool, rel_bias, norm_mix, w_in, b_gate,
           lambda_q1, lambda_k1, lambda_q2, lambda_k2, subln_g, w_pool, pool_scale, w_out,
           norm_ffn, w_gate_up, w_down, norm_final):
    assert norm_mix.shape[0] == 1, "single layer"
    d = D_MODEL
    bp, tp, _ = x_prompt.shape
    bs, ts, _ = x_sample.shape
    past = cache_k.shape[2]
    assert tp % PROJ_ROWS == 0 and tp % FFN_ROWS == 0 and PROJ_ROWS % ATTN_BLOCK == 0
    assert ATTN_BLOCK % CHUNK == 0 and N_HEADS % ATTN_HEADS == 0 and ts >= HALO and ts % 8 == 0
    assert (past + ts - 1) // CHUNK <= past // CHUNK

    w_in_b = w_in.reshape(d, 6 * d).astype(BF16)
    ffn_weights = (norm_mix, w_in_b, b_gate, w_pool[0].astype(BF16), pool_scale,
                   w_out[0].astype(BF16), norm_ffn, w_gate_up[0].astype(BF16),
                   w_down[0].astype(BF16), norm_final.reshape(1, d))
    lam_vecs = jnp.concatenate([lambda_q1, lambda_k1, lambda_q2, lambda_k2], axis=0)

    k_p, v_p, kb, qt, vt = _project(x_prompt, norm_mix, w_in_b, rows=PROJ_ROWS, blocked=True)
    attn_p = _prompt_attention(qt, kb, vt, *_prompt_bias_tiles(rel_bias), lam_vecs,
                               subln_g.reshape(DV, 1))
    y_p, pool_p = _merge_ffn(x_prompt, attn_p, jnp.zeros((bp, HALO, d), F32), ffn_weights,
                             streams=1, rows=FFN_ROWS, pos0=0)

    xs_flat = x_sample.reshape(1, bs * ts, d)
    q_s, k_s, v_s = _project(xs_flat, norm_mix, w_in_b, rows=bs * ts, blocked=False)
    q_s, k_s, v_s = (a.reshape(bs, ts, d) for a in (q_s, k_s, v_s))
    bias_c, bias_n = _sample_bias(rel_bias, past, ts)
    attn_s = _sample_attention(q_s, cache_k.reshape(bs, past * N_HEADS, DV),
                               cache_v.reshape(bs, past * N_HEADS, DV), k_s, v_s, bias_c, bias_n, lam_vecs, subln_g.reshape(1, DV))
    hist_s = jnp.pad(state_pool[0], ((0, 0), (HALO - POOL_HIST, 0), (0, 0)))
    y_s, pool_s = _merge_ffn(x_sample, attn_s, hist_s, ffn_weights, streams=bs, rows=ts, pos0=past)

    heads = (N_HEADS, DV)
    return (y_p, y_s,
            k_p.reshape(1, bp, tp, *heads), v_p.reshape(1, bp, tp, *heads),
            pool_p[None, :, HALO - POOL_HIST:, :],
            k_s.reshape(1, bs, ts, *heads), v_s.reshape(1, bs, ts, *heads),
            pool_s[None, :, HALO - POOL_HIST:, :])
```

```python
import functools
import math

import numpy as np
import jax
import jax.numpy as jnp
from jax import lax
from jax.experimental import pallas as pl
from jax.experimental.pallas import tpu as pltpu

F32 = jnp.float32
BF16 = jnp.bfloat16

D_MODEL = 1024
CHUNK = 64
HEAD_DIM = 64
DV = 2 * HEAD_DIM
N_HEADS = D_MODEL // DV
POOL_WINDOWS = (2, 4, 8, 16)
POOL_GROUP = D_MODEL // len(POOL_WINDOWS)
POOL_HIST = max(POOL_WINDOWS) - 1
HALO = POOL_HIST + 1
D_FF = ((8 * D_MODEL) // 3 + 255) // 256 * 256
NUM_BUCKETS = 32
MAX_DISTANCE = 128
EPS = 1e-6
SUBLN_EPS = 1e-5
LAM_INIT = 0.8 - 0.6 * math.exp(-0.3 * 0)
LOG2E = math.log2(math.e)
QK_SCALE_LOG2 = HEAD_DIM ** -0.5 * LOG2E
MASKED = -1e30

ATTN_BLOCK = 512
ATTN_HEADS = 4
PROJ_ROWS = 1024
FFN_ROWS = 512
FAR_GROUP = 4
BIAS_CORNER = 128
SAMPLE_NEAR = 128
VMEM_LIMIT = 56 * 1024 * 1024
SUBLANES = 8


def _rmsnorm(x, g, eps):
    return x * lax.rsqrt(jnp.mean(x * x, axis=-1, keepdims=True) + eps) * g


def _const_spec(shape, index=None):
    index = (0,) * len(shape) if index is None else index
    return pl.BlockSpec(shape, lambda *_: index, pipeline_mode=pl.Buffered(1))


def _proj_kernel(x_ref, g_ref, w_ref, *out_refs, rows, blocked):
    x = x_ref[0]
    h = _rmsnorm(x, g_ref[...], EPS).astype(BF16)
    q = jnp.dot(h, w_ref[:, 0:D_MODEL], preferred_element_type=F32) * QK_SCALE_LOG2
    k = jnp.dot(h, w_ref[:, D_MODEL:2 * D_MODEL], preferred_element_type=F32)
    v = jnp.dot(h, w_ref[:, 2 * D_MODEL:3 * D_MODEL], preferred_element_type=F32)
    if not blocked:
        q_ref, k_ref, v_ref = out_refs
        q_ref[0] = q
        k_ref[0] = k
        v_ref[0] = v
        return
    k_ref, v_ref, kb_ref, qt_ref, vt_ref = out_refs
    qt = q.T
    vt = v.T
    for hd in range(N_HEADS):
        hs = slice(hd * DV, (hd + 1) * DV)
        k_ref[0, pl.ds(hd, rows, stride=N_HEADS), :] = k[:, hs]
        v_ref[0, pl.ds(hd, rows, stride=N_HEADS), :] = v[:, hs]
        for c in range(rows // ATTN_BLOCK):
            cs = slice(c * ATTN_BLOCK, (c + 1) * ATTN_BLOCK)
            kb_ref[0, hd, c] = k[cs, hs].astype(BF16)
            qt_ref[0, hd, c] = qt[hs, cs].astype(BF16)
            vt_ref[0, hd, c] = vt[hs, cs].astype(BF16)


def _project(x, norm_mix, w_in_b, *, rows, blocked):
    b, t, d = x.shape
    nb_step = rows // ATTN_BLOCK
    nat = jax.ShapeDtypeStruct((b, t, d), F32)
    nat_spec = pl.BlockSpec((1, rows, d), lambda i, j: (i, j, 0))
    if blocked:
        nb = t // ATTN_BLOCK
        per_head = jax.ShapeDtypeStruct((b, t * N_HEADS, DV), F32)
        per_head_spec = pl.BlockSpec((1, rows * N_HEADS, DV), lambda i, j: (i, j, 0))
        out_shape = [per_head, per_head,
                     jax.ShapeDtypeStruct((b, N_HEADS, nb, ATTN_BLOCK, DV), BF16),
                     jax.ShapeDtypeStruct((b, N_HEADS, nb, DV, ATTN_BLOCK), BF16),
                     jax.ShapeDtypeStruct((b, N_HEADS, nb, DV, ATTN_BLOCK), BF16)]
        out_specs = [per_head_spec, per_head_spec,
                     pl.BlockSpec((1, N_HEADS, nb_step, ATTN_BLOCK, DV), lambda i, j: (i, 0, j, 0, 0)),
                     pl.BlockSpec((1, N_HEADS, nb_step, DV, ATTN_BLOCK), lambda i, j: (i, 0, j, 0, 0)),
                     pl.BlockSpec((1, N_HEADS, nb_step, DV, ATTN_BLOCK), lambda i, j: (i, 0, j, 0, 0))]
    else:
        out_shape = [nat, nat, nat]
        out_specs = [nat_spec, nat_spec, nat_spec]
    return pl.pallas_call(
        functools.partial(_proj_kernel, rows=rows, blocked=blocked),
        grid=(b, t // rows),
        in_specs=[nat_spec, _const_spec((1, d)), _const_spec((d, 3 * d))],
        out_specs=out_specs,
        out_shape=out_shape,
        compiler_params=pltpu.CompilerParams(
            dimension_semantics=("arbitrary", "arbitrary"), vmem_limit_bytes=VMEM_LIMIT),
        name="qkv_proj_blocked" if blocked else "qkv_proj",
    )(x, norm_mix, w_in_b)


def _rel_bucket_np(rel):
    nb = NUM_BUCKETS // 2
    ret = np.where(rel > 0, nb, 0)
    n = np.abs(rel)
    max_exact = nb // 2
    large = max_exact + (np.log(np.maximum(n, 1).astype(np.float64) / max_exact)
                         / math.log(MAX_DISTANCE / max_exact) * (nb - max_exact)).astype(np.int64)
    large = np.minimum(large, nb - 1)
    return (ret + np.where(n < max_exact, n, large)).astype(np.int32)


FAR_BUCKET = int(_rel_bucket_np(np.array([-MAX_DISTANCE]))[0])
FAR_DISTANCE = int(np.min(np.nonzero(_rel_bucket_np(-np.arange(4 * MAX_DISTANCE)) == FAR_BUCKET)[0]))
assert np.all(_rel_bucket_np(-np.arange(FAR_DISTANCE, 1 << 16)) == FAR_BUCKET)


def _toeplitz(vec, rows, cols):
    h, length = vec.shape
    assert length == rows + cols - 1
    ext = jnp.concatenate([vec, jnp.zeros((h, 1), vec.dtype)], axis=1)
    skew = jnp.tile(ext, (1, rows))[:, :rows * length].reshape(h, rows, length)
    return skew[:, :, rows - 1:rows - 1 + cols]


def _bias_rows(rel_bias, rels):
    table = (rel_bias - rel_bias[FAR_BUCKET][None, :]).T * LOG2E
    return jnp.take(table, jnp.asarray(_rel_bucket_np(np.asarray(rels))), axis=1)


def _lambda_from(lv):
    s1 = jnp.sum(lv[0:1] * lv[1:2], axis=-1, keepdims=True)
    s2 = jnp.sum(lv[2:3] * lv[3:4], axis=-1, keepdims=True)
    return jnp.exp(s1) - jnp.exp(s2) + LAM_INIT


def _flash_kernel(qt_ref, kb_ref, vt_ref, bdiag_ref, bcorner_ref, lv_ref, g_ref, o_ref,
                  w_ref, m_ref, acc_ref, l_ref, s_ref, bias_ref, *, blk, heads):
    i = pl.program_id(2)
    half = blk // 2
    cn = BIAS_CORNER

    @pl.when(i == 0)
    def _():
        for hd in range(heads):
            corner = bcorner_ref[hd]
            diag = bdiag_ref[hd]
            bias_ref[hd, 0] = jnp.zeros((blk, blk), F32)
            bias_ref[hd, 0, blk - cn:blk, 0:cn] = corner
            bias_ref[hd, 1, 0:half, 0:half] = diag
            bias_ref[hd, 1, half:blk, half:blk] = diag
            bias_ref[hd, 1, 0:half, half:blk] = jnp.zeros((half, half), F32)
            bias_ref[hd, 1, half - cn:half, half:half + cn] = corner
            bias_ref[hd, 1, half:blk, 0:half] = jnp.full((half, half), MASKED, F32)

    zeros = jnp.zeros((HEAD_DIM, blk), BF16)
    for hd in range(heads):
        qt = qt_ref[0, hd, 0]
        w_ref[hd, 0] = jnp.concatenate([qt[:HEAD_DIM], zeros], axis=0)
        w_ref[hd, 1] = jnp.concatenate([zeros, qt[HEAD_DIM:]], axis=0)

    chains = [(hd, mp) for hd in range(heads) for mp in range(2)]

    def key_sums(p):
        return jnp.sum(p.astype(F32).reshape(p.shape[0] // SUBLANES, SUBLANES, p.shape[1]), axis=0)

    def diagonal_step(lookahead):
        def diag_scores(hd, mp):
            sa = jnp.dot(kb_ref[0, hd, i, 0:half, :], w_ref[hd, mp],
                         preferred_element_type=F32) + bias_ref[hd, 1, 0:half, :]
            sb = jnp.dot(kb_ref[0, hd, i, half:blk, :], w_ref[hd, mp, :, half:blk],
                         preferred_element_type=F32) + bias_ref[hd, 1, half:blk, half:blk]
            return sa, sb

        def diag_consume(hd, mp, sa, sb):
            m_r = jnp.maximum(jnp.max(sa[:, half:blk], axis=0, keepdims=True),
                              jnp.max(sb, axis=0, keepdims=True))
            m = jnp.concatenate([jnp.max(sa[:, 0:half], axis=0, keepdims=True), m_r], axis=1)
            m_ref[hd, mp] = m
            pa = jnp.exp2((sa - m).astype(BF16))
            pb = jnp.exp2((sb - m_r).astype(BF16))
            acc_ref[hd, mp] = jnp.dot(vt_ref[0, hd, i, :, 0:half], pa, preferred_element_type=F32)
            acc_ref[hd, mp, :, half:blk] += jnp.dot(vt_ref[0, hd, i, :, half:blk], pb,
                                                    preferred_element_type=F32)
            l_ref[hd, mp] = key_sums(pa)
            l_ref[hd, mp, :, half:blk] += key_sums(pb)

        pending = diag_scores(*chains[0])
        out = None
        for n, (hd, mp) in enumerate(chains):
            if n + 1 < len(chains):
                nxt = diag_scores(*chains[n + 1])
            else:
                nxt = None
                out = lookahead()
            diag_consume(hd, mp, *pending)
            pending = nxt
        return out

    def scores(j, tile, hd, mp):
        s = jnp.dot(kb_ref[0, hd, j], w_ref[hd, mp], preferred_element_type=F32)
        if tile is not None:
            s = s + bias_ref[hd, tile]
        return s

    def consume(j, hd, mp, s):
        p = jnp.exp2((s - m_ref[hd, mp]).astype(BF16))
        acc_ref[hd, mp] += jnp.dot(vt_ref[0, hd, j], p, preferred_element_type=F32)
        l_ref[hd, mp] += key_sums(p)

    def fast_step(j, tile, pending=None, lookahead=None):
        if pending is None:
            pending = scores(j, tile, *chains[0])
        out = None
        for n, (hd, mp) in enumerate(chains):
            if n + 1 < len(chains):
                nxt = scores(j, tile, *chains[n + 1])
            else:
                nxt = None
                out = lookahead() if lookahead is not None else None
            consume(j, hd, mp, pending)
            pending = nxt
        return out

    n_far = jnp.maximum(i - 1, 0)

    def far_scores(j):
        return scores(j, None, *chains[0])

    s_ref[...] = diagonal_step(lambda: scores(jnp.maximum(i - 1, 0), 0, *chains[0]))

    @pl.when(i >= 1)
    def _():
        s_ref[...] = fast_step(i - 1, 0, pending=s_ref[...], lookahead=lambda: far_scores(0))

    def far_group(j0, count):
        pending = s_ref[...]
        for c in range(count):
            nxt_block = jnp.minimum(j0 + c + 1, n_far - 1)
            pending = fast_step(j0 + c, None, pending=pending,
                                lookahead=lambda nb=nxt_block: far_scores(nb))
        s_ref[...] = pending

    @pl.loop(0, n_far // FAR_GROUP)
    def _(jj):
        far_group(FAR_GROUP * jj, FAR_GROUP)

    done = (n_far // FAR_GROUP) * FAR_GROUP
    size = FAR_GROUP // 2
    while size >= 1:
        @pl.when(((n_far - done) & size) != 0)
        def _(done=done, size=size):
            far_group(done, size)
        done = done + ((n_far - done) & size)
        size //= 2

    def safe_recompute():
        @pl.loop(0, 2 * heads)
        def _(c):
            hd = c // 2
            mp = c % 2
            m_ref[hd, mp] = jnp.full((1, blk), MASKED, F32)
            acc_ref[hd, mp] = jnp.zeros((DV, blk), F32)
            l_ref[hd, mp] = jnp.zeros((SUBLANES, blk), F32)

            @pl.loop(0, i + 1)
            def _(j):
                s = jnp.dot(kb_ref[0, hd, j], w_ref[hd, mp], preferred_element_type=F32)
                tile = jnp.clip(j - (i - 1), 0, 1)
                s = s + jnp.where(j >= i - 1, bias_ref[hd, tile], 0.0)
                m_old = m_ref[hd, mp]
                m_new = jnp.maximum(m_old, jnp.max(s, axis=0, keepdims=True))
                alpha = jnp.exp2(m_old - m_new)
                p = jnp.exp2((s - m_new).astype(BF16))
                m_ref[hd, mp] = m_new
                pv = jnp.dot(vt_ref[0, hd, j], p, preferred_element_type=F32)
                acc_ref[hd, mp] = alpha * acc_ref[hd, mp] + pv
                l_ref[hd, mp] = alpha * l_ref[hd, mp] + key_sums(p)

    lam = _lambda_from(lv_ref[...])

    def finalize():
        poison = jnp.zeros((1, 1), F32)
        for hd in range(heads):
            l1 = jnp.sum(l_ref[hd, 0], axis=0, keepdims=True)
            l2 = jnp.sum(l_ref[hd, 1], axis=0, keepdims=True)
            o = acc_ref[hd, 0] * (1.0 / l1) - lam * (acc_ref[hd, 1] * (1.0 / l2))
            ms = jnp.mean(o * o, axis=0, keepdims=True)
            y = o * (lax.rsqrt(ms + SUBLN_EPS) * (1.0 - LAM_INIT)) * g_ref[...]
            o_ref[0, hd * DV:(hd + 1) * DV, :] = y
            poison = poison + jnp.sum((ms + l1 + l2) * 0.0, axis=1, keepdims=True)
        return poison

    poison = finalize()

    @pl.when(jnp.logical_not(poison[0, 0] == 0.0))
    def _():
        safe_recompute()
        finalize()


def _prompt_bias_tiles(rel_bias):
    half = ATTN_BLOCK // 2
    cn = BIAS_CORNER
    assert cn >= FAR_DISTANCE - 1 and half >= cn and half % CHUNK == 0
    j = np.arange(2 * half - 1)
    diag = _toeplitz(_bias_rows(rel_bias, half - 1 - j), half, half)
    kk = np.arange(half)[:, None]
    qq = np.arange(half)[None, :]
    visible = jnp.asarray((kk // CHUNK) <= (qq // CHUNK))
    j = np.arange(2 * cn - 1)
    corner = _toeplitz(_bias_rows(rel_bias, -1 - j), cn, cn)
    return jnp.where(visible[None], diag, MASKED), corner


def _prompt_attention(qt, kb, vt, bias_diag, bias_corner, lam_vecs, subln_col):
    b, nh, nb, _, blk = qt.shape
    t = nb * blk
    hps = ATTN_HEADS
    return pl.pallas_call(
        functools.partial(_flash_kernel, blk=blk, heads=hps),
        grid=(b, nh // hps, nb),
        in_specs=[
            pl.BlockSpec((1, hps, 1, DV, blk), lambda bi, h, i: (bi, h, i, 0, 0)),
            pl.BlockSpec((1, hps, nb, blk, DV), lambda bi, h, i: (bi, h, 0, 0, 0)),
            pl.BlockSpec((1, hps, nb, DV, blk), lambda bi, h, i: (bi, h, 0, 0, 0)),
            pl.BlockSpec((hps, blk // 2, blk // 2), lambda bi, h, i: (h, 0, 0),
                         pipeline_mode=pl.Buffered(1)),
            pl.BlockSpec((hps, BIAS_CORNER, BIAS_CORNER), lambda bi, h, i: (h, 0, 0),
                         pipeline_mode=pl.Buffered(1)),
            _const_spec((4, HEAD_DIM)),
            _const_spec((DV, 1)),
        ],
        out_specs=pl.BlockSpec((1, hps * DV, blk), lambda bi, h, i: (bi, h, i)),
        out_shape=jax.ShapeDtypeStruct((b, nh * DV, t), F32),
        scratch_shapes=[pltpu.VMEM((hps, 2, DV, blk), BF16),
                        pltpu.VMEM((hps, 2, 1, blk), F32),
                        pltpu.VMEM((hps, 2, DV, blk), F32),
                        pltpu.VMEM((hps, 2, SUBLANES, blk), F32),
                        pltpu.VMEM((blk, blk), F32),
                        pltpu.VMEM((hps, 2, blk, blk), F32)],
        compiler_params=pltpu.CompilerParams(
            dimension_semantics=("arbitrary", "arbitrary", "arbitrary"),
            vmem_limit_bytes=VMEM_LIMIT),
        name="prompt_attention",
    )(qt, kb, vt, bias_diag, bias_corner, lam_vecs, subln_col)


def _sample_attn_kernel(q_ref, ck_ref, cv_ref, kn_ref, vn_ref, bc_ref, bn_ref, lv_ref, g_ref, o_ref,
                        *, past, far):
    lam = _lambda_from(lv_ref[...])
    q = q_ref[0]
    kn = kn_ref[0]
    vn = vn_ref[0]
    s_new = q.shape[0]
    lane = lax.broadcasted_iota(jnp.int32, (s_new, DV), 1)
    nt = (((1,), (1,)), ((), ()))

    def head_scores(hd):
        hs = slice(hd * DV, (hd + 1) * DV)
        qh = q[:, hs]
        qq = jnp.concatenate([jnp.where(lane < HEAD_DIM, qh, 0.0),
                              jnp.where(lane >= HEAD_DIM, qh, 0.0)], axis=0).astype(BF16)
        kc = ck_ref[0, pl.ds(hd, past, stride=N_HEADS), :].astype(BF16)
        sf = lax.dot_general(qq, kc[:far], nt, preferred_element_type=F32)
        sc = lax.dot_general(qq, kc[far:], nt, preferred_element_type=F32) + bc_ref[hd]
        sn = lax.dot_general(qq, kn[:, hs].astype(BF16), nt, preferred_element_type=F32) + bn_ref[hd]
        return sf, sc, sn

    def head_output(hd, sf, sc, sn):
        hs = slice(hd * DV, (hd + 1) * DV)
        vc = cv_ref[0, pl.ds(hd, past, stride=N_HEADS), :].astype(BF16)
        m = jnp.maximum(jnp.maximum(jnp.max(sf, axis=-1, keepdims=True),
                                    jnp.max(sc, axis=-1, keepdims=True)),
                        jnp.max(sn, axis=-1, keepdims=True))
        pf = jnp.exp2(sf - m)
        pc = jnp.exp2(sc - m)
        pn = jnp.exp2(sn - m)
        l = (jnp.sum(pf, axis=-1, keepdims=True) + jnp.sum(pc, axis=-1, keepdims=True)
             + jnp.sum(pn, axis=-1, keepdims=True))
        o = (jnp.dot(pf.astype(BF16), vc[:far], preferred_element_type=F32)
             + jnp.dot(pc.astype(BF16), vc[far:], preferred_element_type=F32)
             + jnp.dot(pn.astype(BF16), vn[:, hs].astype(BF16), preferred_element_type=F32))
        o = o * (1.0 / l)
        o = o[0:s_new] - lam * o[s_new:2 * s_new]
        o_ref[0, :, hs] = _rmsnorm(o, g_ref[...], SUBLN_EPS) * (1.0 - LAM_INIT)

    pending = head_scores(0)
    for hd in range(N_HEADS):
        nxt = head_scores(hd + 1) if hd + 1 < N_HEADS else None
        head_output(hd, *pending)
        pending = nxt


def _sample_bias(rel_bias, past, s_new):
    near = SAMPLE_NEAR
    j = np.arange(s_new + near - 1)
    bias_c = _toeplitz(_bias_rows(rel_bias, j - (s_new - 1) - near), s_new, near)
    j = np.arange(2 * s_new - 1)
    bias_n = _toeplitz(_bias_rows(rel_bias, j - (s_new - 1)), s_new, s_new)
    twice = lambda t: jnp.concatenate([t, t], axis=1)
    return twice(bias_c), twice(bias_n)


def _sample_attention(q, cache_k, cache_v, k_new, v_new, bias_c, bias_n, lam_vecs, subln_row):
    b, s, d = q.shape
    past = cache_k.shape[1] // N_HEADS
    far = past - SAMPLE_NEAR
    assert far >= 0 and SAMPLE_NEAR >= FAR_DISTANCE
    new_spec = pl.BlockSpec((1, s, d), lambda i: (i, 0, 0))
    cache_spec = pl.BlockSpec((1, past * N_HEADS, DV), lambda i: (i, 0, 0))
    return pl.pallas_call(
        functools.partial(_sample_attn_kernel, past=past, far=far),
        grid=(b,),
        in_specs=[new_spec, cache_spec, cache_spec, new_spec, new_spec,
                  _const_spec(bias_c.shape), _const_spec(bias_n.shape),
                  _const_spec((4, HEAD_DIM)), _const_spec((1, DV))],
        out_specs=new_spec,
        out_shape=jax.ShapeDtypeStruct((b, s, d), F32),
        compiler_params=pltpu.CompilerParams(
            dimension_semantics=("arbitrary",), vmem_limit_bytes=VMEM_LIMIT),
        name="sample_attention",
    )(q, cache_k, cache_v, k_new, v_new, bias_c, bias_n, lam_vecs, subln_row)


def _ffn_kernel(x_ref, a_ref, hist_ref, nm_ref, wug_ref, bg_ref, wp_ref, ps_ref, wo_ref,
                nf_ref, wgu_ref, wd_ref, nfin_ref, y_ref, pst_ref, carry_ref, ext_ref,
                *, streams, rows, pos0, attn_transposed):
    t = pl.program_id(1)
    n = streams * rows
    d = D_MODEL

    @pl.when(t == 0)
    def _():
        carry_ref[...] = hist_ref[...]

    x = x_ref[...].reshape(n, d)
    h = _rmsnorm(x, nm_ref[...], EPS).astype(BF16)
    u = jnp.dot(h, wug_ref[:, 0:d], preferred_element_type=F32)
    zg = jnp.dot(h, wug_ref[:, d:3 * d], preferred_element_type=F32) + bg_ref[...]

    ext_ref[:, 0:HALO, :] = carry_ref[...]
    ext_ref[:, HALO:HALO + rows, :] = u.reshape(streams, rows, d)
    last = ext_ref[:, rows:rows + HALO, :]
    carry_ref[...] = last
    pst_ref[...] = last

    pos = pos0 + t * rows + lax.broadcasted_iota(jnp.int32, (1, rows, 1), 1)
    pooled = []
    for gi, w in enumerate(POOL_WINDOWS):
        cs = slice(gi * POOL_GROUP, (gi + 1) * POOL_GROUP)
        cur = ext_ref[:, HALO:HALO + rows, cs]
        s = cur
        for back in range(1, w):
            s = s + ext_ref[:, HALO - back:HALO - back + rows, cs]
        inv = 1.0 / jnp.minimum(pos + 1, w).astype(F32)
        dlt = (s * inv - cur).reshape(n, POOL_GROUP).astype(BF16)
        pooled.append(jnp.dot(dlt, wp_ref[gi], preferred_element_type=F32))
    pool = jnp.concatenate(pooled, axis=-1) * ps_ref[...]

    g_a = jax.nn.sigmoid(zg[:, 0:d])
    g_p = jax.nn.sigmoid(zg[:, d:2 * d])
    attn = a_ref[0].T if attn_transposed else a_ref[...].reshape(n, d)
    merged = g_a * attn + g_p * pool
    x1 = x + jnp.dot(merged.astype(BF16), wo_ref[...], preferred_element_type=F32)

    h2 = _rmsnorm(x1, nf_ref[...], EPS).astype(BF16)
    gu = jnp.dot(h2, wgu_ref[...], preferred_element_type=F32)
    act = jax.nn.silu(gu[:, 0:D_FF]) * gu[:, D_FF:2 * D_FF]
    x2 = x1 + jnp.dot(act.astype(BF16), wd_ref[...], preferred_element_type=F32)
    y_ref[...] = _rmsnorm(x2, nfin_ref[...], EPS).reshape(streams, rows, d)


def _merge_ffn(x, attn, hist, weights, *, streams, rows, pos0, attn_transposed):
    b, t, d = x.shape
    tok_spec = pl.BlockSpec((streams, rows, d), lambda i, j: (i, j, 0))
    halo_spec = pl.BlockSpec((streams, HALO, d), lambda i, j: (i, 0, 0))
    if attn_transposed:
        assert streams == 1
        attn_spec = pl.BlockSpec((1, d, rows), lambda i, j: (i, 0, j))
    else:
        attn_spec = tok_spec
    w_specs = [_const_spec(w.shape) for w in weights]
    w_specs[1] = _const_spec((d, 3 * d), index=(0, 1))
    return pl.pallas_call(
        functools.partial(_ffn_kernel, streams=streams, rows=rows, pos0=pos0,
                          attn_transposed=attn_transposed),
        grid=(b // streams, t // rows),
        in_specs=[tok_spec, attn_spec, halo_spec] + w_specs,
        out_specs=[tok_spec, halo_spec],
        out_shape=[jax.ShapeDtypeStruct((b, t, d), F32),
                   jax.ShapeDtypeStruct((b, HALO, d), F32)],
        scratch_shapes=[pltpu.VMEM((streams, HALO, d), F32),
                        pltpu.VMEM((streams, HALO + rows, d), F32)],
        compiler_params=pltpu.CompilerParams(
            dimension_semantics=("arbitrary", "arbitrary"), vmem_limit_bytes=VMEM_LIMIT),
        name="merge_ffn",
    )(x, attn, hist, *weights)


def kernel(x_prompt, x_sample, cache_k, cache_v, state_pool, rel_bias, norm_mix, w_in, b_gate,
           lambda_q1, lambda_k1, lambda_q2, lambda_k2, subln_g, w_pool, pool_scale, w_out,
           norm_ffn, w_gate_up, w_down, norm_final):
    assert norm_mix.shape[0] == 1, "single layer"
    d = D_MODEL
    bp, tp, _ = x_prompt.shape
    bs, ts, _ = x_sample.shape
    past = cache_k.shape[2]
    assert tp % PROJ_ROWS == 0 and tp % FFN_ROWS == 0 and PROJ_ROWS % ATTN_BLOCK == 0
    assert ATTN_BLOCK % CHUNK == 0 and N_HEADS % ATTN_HEADS == 0 and ts >= HALO and ts % 8 == 0
    assert (past + ts - 1) // CHUNK <= past // CHUNK

    w_in_b = w_in.reshape(d, 6 * d).astype(BF16)
    ffn_weights = (norm_mix, w_in_b, b_gate, w_pool[0].astype(BF16), pool_scale,
                   w_out[0].astype(BF16), norm_ffn, w_gate_up[0].astype(BF16),
                   w_down[0].astype(BF16), norm_final.reshape(1, d))
    lam_vecs = jnp.concatenate([lambda_q1, lambda_k1, lambda_q2, lambda_k2], axis=0)

    k_p, v_p, kb, qt, vt = _project(x_prompt, norm_mix, w_in_b, rows=PROJ_ROWS, blocked=True)
    attn_p = _prompt_attention(qt, kb, vt, *_prompt_bias_tiles(rel_bias), lam_vecs,
                               subln_g.reshape(DV, 1))
    y_p, pool_p = _merge_ffn(x_prompt, attn_p, jnp.zeros((bp, HALO, d), F32), ffn_weights,
                             streams=1, rows=FFN_ROWS, pos0=0, attn_transposed=True)

    xs_flat = x_sample.reshape(1, bs * ts, d)
    q_s, k_s, v_s = _project(xs_flat, norm_mix, w_in_b, rows=bs * ts, blocked=False)
    q_s, k_s, v_s = (a.reshape(bs, ts, d) for a in (q_s, k_s, v_s))
    bias_c, bias_n = _sample_bias(rel_bias, past, ts)
    attn_s = _sample_attention(q_s, cache_k.reshape(bs, past * N_HEADS, DV),
                               cache_v.reshape(bs, past * N_HEADS, DV), k_s, v_s, bias_c, bias_n, lam_vecs, subln_g.reshape(1, DV))
    hist_s = jnp.pad(state_pool[0], ((0, 0), (HALO - POOL_HIST, 0), (0, 0)))
    y_s, pool_s = _merge_ffn(x_sample, attn_s, hist_s, ffn_weights, streams=bs, rows=ts, pos0=past,
                             attn_transposed=False)

    heads = (N_HEADS, DV)
    return (y_p, y_s,
            k_p.reshape(1, bp, tp, *heads), v_p.reshape(1, bp, tp, *heads),
            pool_p[None, :, HALO - POOL_HIST:, :],
            k_s.reshape(1, bs, ts, *heads), v_s.reshape(1, bs, ts, *heads),
            pool_s[None, :, HALO - POOL_HIST:, :])
```

```python
import functools
import math

import numpy as np
import jax
import jax.numpy as jnp
from jax import lax
from jax.experimental import pallas as pl
from jax.experimental.pallas import tpu as pltpu

F32 = jnp.float32
BF16 = jnp.bfloat16

D_MODEL = 1024
CHUNK = 64
HEAD_DIM = 64
DV = 2 * HEAD_DIM
N_HEADS = D_MODEL // DV
POOL_WINDOWS = (2, 4, 8, 16)
POOL_GROUP = D_MODEL // len(POOL_WINDOWS)
POOL_HIST = max(POOL_WINDOWS) - 1
HALO = POOL_HIST + 1
D_FF = ((8 * D_MODEL) // 3 + 255) // 256 * 256
NUM_BUCKETS = 32
MAX_DISTANCE = 128
EPS = 1e-6
SUBLN_EPS = 1e-5
LAM_INIT = 0.8 - 0.6 * math.exp(-0.3 * 0)
LOG2E = math.log2(math.e)
QK_SCALE_LOG2 = HEAD_DIM ** -0.5 * LOG2E
MASKED = -1e30
SUM_LIMIT = 2.0 ** 16

ATTN_BLOCK = 512
ATTN_HEADS = 4
PROJ_ROWS = 1024
FFN_ROWS = 512
FAR_GROUP = 4
BIAS_CORNER = 128
SAMPLE_NEAR = 128
VMEM_LIMIT = 56 * 1024 * 1024
SUBLANES = 8


def _rmsnorm(x, g, eps):
    return x * lax.rsqrt(jnp.mean(x * x, axis=-1, keepdims=True) + eps) * g


def _const_spec(shape, index=None):
    index = (0,) * len(shape) if index is None else index
    return pl.BlockSpec(shape, lambda *_: index, pipeline_mode=pl.Buffered(1))


def _proj_kernel(x_ref, g_ref, w_ref, *out_refs, rows, blocked):
    x = x_ref[0]
    h = _rmsnorm(x, g_ref[...], EPS).astype(BF16)
    q = jnp.dot(h, w_ref[:, 0:D_MODEL], preferred_element_type=F32) * QK_SCALE_LOG2
    k = jnp.dot(h, w_ref[:, D_MODEL:2 * D_MODEL], preferred_element_type=F32)
    v = jnp.dot(h, w_ref[:, 2 * D_MODEL:3 * D_MODEL], preferred_element_type=F32)
    if not blocked:
        q_ref, k_ref, v_ref = out_refs
        q_ref[0] = q
        k_ref[0] = k
        v_ref[0] = v
        return
    k_ref, v_ref, kb_ref, qt_ref, vt_ref = out_refs
    qt = q.T
    vt = v.T
    for hd in range(N_HEADS):
        hs = slice(hd * DV, (hd + 1) * DV)
        k_ref[0, pl.ds(hd, rows, stride=N_HEADS), :] = k[:, hs]
        v_ref[0, pl.ds(hd, rows, stride=N_HEADS), :] = v[:, hs]
        for c in range(rows // ATTN_BLOCK):
            cs = slice(c * ATTN_BLOCK, (c + 1) * ATTN_BLOCK)
            kb_ref[0, hd, c] = k[cs, hs].astype(BF16)
            qt_ref[0, hd, c] = qt[hs, cs].astype(BF16)
            vt_ref[0, hd, c] = vt[hs, cs].astype(BF16)


def _project(x, norm_mix, w_in_b, *, rows, blocked):
    b, t, d = x.shape
    nb_step = rows // ATTN_BLOCK
    nat = jax.ShapeDtypeStruct((b, t, d), F32)
    nat_spec = pl.BlockSpec((1, rows, d), lambda i, j: (i, j, 0))
    if blocked:
        nb = t // ATTN_BLOCK
        per_head = jax.ShapeDtypeStruct((b, t * N_HEADS, DV), F32)
        per_head_spec = pl.BlockSpec((1, rows * N_HEADS, DV), lambda i, j: (i, j, 0))
        out_shape = [per_head, per_head,
                     jax.ShapeDtypeStruct((b, N_HEADS, nb, ATTN_BLOCK, DV), BF16),
                     jax.ShapeDtypeStruct((b, N_HEADS, nb, DV, ATTN_BLOCK), BF16),
                     jax.ShapeDtypeStruct((b, N_HEADS, nb, DV, ATTN_BLOCK), BF16)]
        out_specs = [per_head_spec, per_head_spec,
                     pl.BlockSpec((1, N_HEADS, nb_step, ATTN_BLOCK, DV), lambda i, j: (i, 0, j, 0, 0)),
                     pl.BlockSpec((1, N_HEADS, nb_step, DV, ATTN_BLOCK), lambda i, j: (i, 0, j, 0, 0)),
                     pl.BlockSpec((1, N_HEADS, nb_step, DV, ATTN_BLOCK), lambda i, j: (i, 0, j, 0, 0))]
    else:
        out_shape = [nat, nat, nat]
        out_specs = [nat_spec, nat_spec, nat_spec]
    return pl.pallas_call(
        functools.partial(_proj_kernel, rows=rows, blocked=blocked),
        grid=(b, t // rows),
        in_specs=[nat_spec, _const_spec((1, d)), _const_spec((d, 3 * d))],
        out_specs=out_specs,
        out_shape=out_shape,
        compiler_params=pltpu.CompilerParams(
            dimension_semantics=("arbitrary", "arbitrary"), vmem_limit_bytes=VMEM_LIMIT),
        name="qkv_proj_blocked" if blocked else "qkv_proj",
    )(x, norm_mix, w_in_b)


def _rel_bucket_np(rel):
    nb = NUM_BUCKETS // 2
    ret = np.where(rel > 0, nb, 0)
    n = np.abs(rel)
    max_exact = nb // 2
    large = max_exact + (np.log(np.maximum(n, 1).astype(np.float64) / max_exact)
                         / math.log(MAX_DISTANCE / max_exact) * (nb - max_exact)).astype(np.int64)
    large = np.minimum(large, nb - 1)
    return (ret + np.where(n < max_exact, n, large)).astype(np.int32)


FAR_BUCKET = int(_rel_bucket_np(np.array([-MAX_DISTANCE]))[0])
FAR_DISTANCE = int(np.min(np.nonzero(_rel_bucket_np(-np.arange(4 * MAX_DISTANCE)) == FAR_BUCKET)[0]))
assert np.all(_rel_bucket_np(-np.arange(FAR_DISTANCE, 1 << 16)) == FAR_BUCKET)


def _toeplitz(vec, rows, cols):
    h, length = vec.shape
    assert length == rows + cols - 1
    ext = jnp.concatenate([vec, jnp.zeros((h, 1), vec.dtype)], axis=1)
    skew = jnp.tile(ext, (1, rows))[:, :rows * length].reshape(h, rows, length)
    return skew[:, :, rows - 1:rows - 1 + cols]


def _bias_rows(rel_bias, rels):
    table = (rel_bias - rel_bias[FAR_BUCKET][None, :]).T * LOG2E
    return jnp.take(table, jnp.asarray(_rel_bucket_np(np.asarray(rels))), axis=1)


def _lambda_from(lv):
    s1 = jnp.sum(lv[0:1] * lv[1:2], axis=-1, keepdims=True)
    s2 = jnp.sum(lv[2:3] * lv[3:4], axis=-1, keepdims=True)
    return jnp.exp(s1) - jnp.exp(s2) + LAM_INIT


def _flash_kernel(qt_ref, kb_ref, vt_ref, bdiag_ref, bcorner_ref, lv_ref, g_ref, o_ref,
                  w_ref, m_ref, acc_ref, l_ref, s_ref, bias_ref, *, blk, heads):
    i = pl.program_id(2)
    half = blk // 2
    cn = BIAS_CORNER

    @pl.when(i == 0)
    def _():
        for hd in range(heads):
            corner = bcorner_ref[hd]
            diag = bdiag_ref[hd]
            bias_ref[hd, 0] = jnp.zeros((blk, blk), F32)
            bias_ref[hd, 0, blk - cn:blk, 0:cn] = corner
            bias_ref[hd, 1, 0:half, 0:half] = diag
            bias_ref[hd, 1, half:blk, half:blk] = diag
            bias_ref[hd, 1, 0:half, half:blk] = jnp.zeros((half, half), F32)
            bias_ref[hd, 1, half - cn:half, half:half + cn] = corner
            bias_ref[hd, 1, half:blk, 0:half] = jnp.full((half, half), MASKED, F32)

    zeros = jnp.zeros((HEAD_DIM, blk), BF16)
    for hd in range(heads):
        qt = qt_ref[0, hd, 0]
        w_ref[hd, 0] = jnp.concatenate([qt[:HEAD_DIM], zeros], axis=0)
        w_ref[hd, 1] = jnp.concatenate([zeros, qt[HEAD_DIM:]], axis=0)

    chains = [(hd, mp) for hd in range(heads) for mp in range(2)]

    def key_sums(p):
        return jnp.sum(p.astype(F32).reshape(p.shape[0] // SUBLANES, SUBLANES, p.shape[1]), axis=0)

    def diagonal_step(lookahead):
        def diag_scores(hd, mp):
            sa = jnp.dot(kb_ref[0, hd, i, 0:half, :], w_ref[hd, mp],
                         preferred_element_type=F32) + bias_ref[hd, 1, 0:half, :]
            sb = jnp.dot(kb_ref[0, hd, i, half:blk, :], w_ref[hd, mp, :, half:blk],
                         preferred_element_type=F32) + bias_ref[hd, 1, half:blk, half:blk]
            return sa, sb

        def diag_consume(hd, mp, sa, sb):
            m_r = jnp.maximum(jnp.max(sa[:, half:blk], axis=0, keepdims=True),
                              jnp.max(sb, axis=0, keepdims=True))
            m = jnp.concatenate([jnp.max(sa[:, 0:half], axis=0, keepdims=True), m_r], axis=1)
            m_ref[hd, mp] = m
            pa = jnp.exp2((sa - m).astype(BF16))
            pb = jnp.exp2((sb - m_r).astype(BF16))
            acc_ref[hd, mp] = jnp.dot(vt_ref[0, hd, i, :, 0:half], pa, preferred_element_type=F32)
            acc_ref[hd, mp, :, half:blk] += jnp.dot(vt_ref[0, hd, i, :, half:blk], pb,
                                                    preferred_element_type=F32)
            l_ref[hd, mp] = key_sums(pa)
            l_ref[hd, mp, :, half:blk] += key_sums(pb)

        pending = diag_scores(*chains[0])
        out = None
        for n, (hd, mp) in enumerate(chains):
            if n + 1 < len(chains):
                nxt = diag_scores(*chains[n + 1])
            else:
                nxt = None
                out = lookahead()
            diag_consume(hd, mp, *pending)
            pending = nxt
        return out

    def scores(j, tile, hd, mp):
        s = jnp.dot(kb_ref[0, hd, j], w_ref[hd, mp], preferred_element_type=F32)
        if tile is not None:
            s = s + bias_ref[hd, tile]
        return s

    def consume(j, hd, mp, s):
        p = jnp.exp2((s - m_ref[hd, mp]).astype(BF16))
        acc_ref[hd, mp] += jnp.dot(vt_ref[0, hd, j], p, preferred_element_type=F32)
        l_ref[hd, mp] += key_sums(p)

    def fast_step(j, tile, pending=None, lookahead=None):
        if pending is None:
            pending = scores(j, tile, *chains[0])
        out = None
        for n, (hd, mp) in enumerate(chains):
            if n + 1 < len(chains):
                nxt = scores(j, tile, *chains[n + 1])
            else:
                nxt = None
                out = lookahead() if lookahead is not None else None
            consume(j, hd, mp, pending)
            pending = nxt
        return out

    n_far = jnp.maximum(i - 1, 0)

    def far_scores(j):
        return scores(j, None, *chains[0])

    s_ref[...] = diagonal_step(lambda: scores(jnp.maximum(i - 1, 0), 0, *chains[0]))

    @pl.when(i >= 1)
    def _():
        s_ref[...] = fast_step(i - 1, 0, pending=s_ref[...], lookahead=lambda: far_scores(0))

    def far_group(j0, count):
        pending = s_ref[...]
        for c in range(count):
            nxt_block = jnp.minimum(j0 + c + 1, n_far - 1)
            pending = fast_step(j0 + c, None, pending=pending,
                                lookahead=lambda nb=nxt_block: far_scores(nb))
        s_ref[...] = pending

    @pl.loop(0, n_far // FAR_GROUP)
    def _(jj):
        far_group(FAR_GROUP * jj, FAR_GROUP)

    done = (n_far // FAR_GROUP) * FAR_GROUP
    size = FAR_GROUP // 2
    while size >= 1:
        @pl.when(((n_far - done) & size) != 0)
        def _(done=done, size=size):
            far_group(done, size)
        done = done + ((n_far - done) & size)
        size //= 2

    def safe_recompute():
        @pl.loop(0, 2 * heads)
        def _(c):
            hd = c // 2
            mp = c % 2
            m_ref[hd, mp] = jnp.full((1, blk), MASKED, F32)
            acc_ref[hd, mp] = jnp.zeros((DV, blk), F32)
            l_ref[hd, mp] = jnp.zeros((SUBLANES, blk), F32)

            @pl.loop(0, i + 1)
            def _(j):
                s = jnp.dot(kb_ref[0, hd, j], w_ref[hd, mp], preferred_element_type=F32)
                tile = jnp.clip(j - (i - 1), 0, 1)
                s = s + jnp.where(j >= i - 1, bias_ref[hd, tile], 0.0)
                m_old = m_ref[hd, mp]
                m_new = jnp.maximum(m_old, jnp.max(s, axis=0, keepdims=True))
                alpha = jnp.exp2(m_old - m_new)
                p = jnp.exp2((s - m_new).astype(BF16))
                m_ref[hd, mp] = m_new
                pv = jnp.dot(vt_ref[0, hd, j], p, preferred_element_type=F32)
                acc_ref[hd, mp] = alpha * acc_ref[hd, mp] + pv
                l_ref[hd, mp] = alpha * l_ref[hd, mp] + key_sums(p)

    lam = _lambda_from(lv_ref[...])

    def finalize():
        poison = jnp.zeros((1, 1), F32)
        for hd in range(heads):
            l1 = jnp.sum(l_ref[hd, 0], axis=0, keepdims=True)
            l2 = jnp.sum(l_ref[hd, 1], axis=0, keepdims=True)
            o = acc_ref[hd, 0] * (1.0 / l1) - lam * (acc_ref[hd, 1] * (1.0 / l2))
            ms = jnp.mean(o * o, axis=0, keepdims=True)
            y = o * (lax.rsqrt(ms + SUBLN_EPS) * (1.0 - LAM_INIT)) * g_ref[...]
            o_ref[0, hd * DV:(hd + 1) * DV, :] = y
            flags = (ms + l1 + l2) * 0.0 + jnp.where(jnp.maximum(l1, l2) > SUM_LIMIT, jnp.nan, 0.0)
            poison = poison + jnp.sum(flags, axis=1, keepdims=True)
        return poison

    poison = finalize()

    @pl.when(jnp.logical_not(poison[0, 0] == 0.0))
    def _():
        safe_recompute()
        finalize()


def _prompt_bias_tiles(rel_bias):
    half = ATTN_BLOCK // 2
    cn = BIAS_CORNER
    assert cn >= FAR_DISTANCE - 1 and half >= cn and half % CHUNK == 0
    j = np.arange(2 * half - 1)
    diag = _toeplitz(_bias_rows(rel_bias, half - 1 - j), half, half)
    kk = np.arange(half)[:, None]
    qq = np.arange(half)[None, :]
    visible = jnp.asarray((kk // CHUNK) <= (qq // CHUNK))
    j = np.arange(2 * cn - 1)
    corner = _toeplitz(_bias_rows(rel_bias, -1 - j), cn, cn)
    return jnp.where(visible[None], diag, MASKED), corner


def _prompt_attention(qt, kb, vt, bias_diag, bias_corner, lam_vecs, subln_col):
    b, nh, nb, _, blk = qt.shape
    t = nb * blk
    hps = ATTN_HEADS
    return pl.pallas_call(
        functools.partial(_flash_kernel, blk=blk, heads=hps),
        grid=(b, nh // hps, nb),
        in_specs=[
            pl.BlockSpec((1, hps, 1, DV, blk), lambda bi, h, i: (bi, h, i, 0, 0)),
            pl.BlockSpec((1, hps, nb, blk, DV), lambda bi, h, i: (bi, h, 0, 0, 0)),
            pl.BlockSpec((1, hps, nb, DV, blk), lambda bi, h, i: (bi, h, 0, 0, 0)),
            pl.BlockSpec((hps, blk // 2, blk // 2), lambda bi, h, i: (h, 0, 0),
                         pipeline_mode=pl.Buffered(1)),
            pl.BlockSpec((hps, BIAS_CORNER, BIAS_CORNER), lambda bi, h, i: (h, 0, 0),
                         pipeline_mode=pl.Buffered(1)),
            _const_spec((4, HEAD_DIM)),
            _const_spec((DV, 1)),
        ],
        out_specs=pl.BlockSpec((1, hps * DV, blk), lambda bi, h, i: (bi, h, i)),
        out_shape=jax.ShapeDtypeStruct((b, nh * DV, t), F32),
        scratch_shapes=[pltpu.VMEM((hps, 2, DV, blk), BF16),
                        pltpu.VMEM((hps, 2, 1, blk), F32),
                        pltpu.VMEM((hps, 2, DV, blk), F32),
                        pltpu.VMEM((hps, 2, SUBLANES, blk), F32),
                        pltpu.VMEM((blk, blk), F32),
                        pltpu.VMEM((hps, 2, blk, blk), F32)],
        compiler_params=pltpu.CompilerParams(
            dimension_semantics=("arbitrary", "arbitrary", "arbitrary"),
            vmem_limit_bytes=VMEM_LIMIT),
        name="prompt_attention",
    )(qt, kb, vt, bias_diag, bias_corner, lam_vecs, subln_col)


def _sample_attn_kernel(q_ref, ck_ref, cv_ref, kn_ref, vn_ref, bc_ref, bn_ref, lv_ref, g_ref, o_ref,
                        *, past, far):
    lam = _lambda_from(lv_ref[...])
    q = q_ref[0]
    kn = kn_ref[0]
    vn = vn_ref[0]
    s_new = q.shape[0]
    lane = lax.broadcasted_iota(jnp.int32, (s_new, DV), 1)
    nt = (((1,), (1,)), ((), ()))

    def head_scores(hd):
        hs = slice(hd * DV, (hd + 1) * DV)
        qh = q[:, hs]
        qq = jnp.concatenate([jnp.where(lane < HEAD_DIM, qh, 0.0),
                              jnp.where(lane >= HEAD_DIM, qh, 0.0)], axis=0).astype(BF16)
        kc = ck_ref[0, pl.ds(hd, past, stride=N_HEADS), :].astype(BF16)
        sf = lax.dot_general(qq, kc[:far], nt, preferred_element_type=F32)
        sc = lax.dot_general(qq, kc[far:], nt, preferred_element_type=F32) + bc_ref[hd]
        sn = lax.dot_general(qq, kn[:, hs].astype(BF16), nt, preferred_element_type=F32) + bn_ref[hd]
        return sf, sc, sn

    def head_output(hd, sf, sc, sn):
        hs = slice(hd * DV, (hd + 1) * DV)
        vc = cv_ref[0, pl.ds(hd, past, stride=N_HEADS), :].astype(BF16)
        m = jnp.maximum(jnp.maximum(jnp.max(sf, axis=-1, keepdims=True),
                                    jnp.max(sc, axis=-1, keepdims=True)),
                        jnp.max(sn, axis=-1, keepdims=True))
        pf = jnp.exp2(sf - m)
        pc = jnp.exp2(sc - m)
        pn = jnp.exp2(sn - m)
        l = (jnp.sum(pf, axis=-1, keepdims=True) + jnp.sum(pc, axis=-1, keepdims=True)
             + jnp.sum(pn, axis=-1, keepdims=True))
        o = (jnp.dot(pf.astype(BF16), vc[:far], preferred_element_type=F32)
             + jnp.dot(pc.astype(BF16), vc[far:], preferred_element_type=F32)
             + jnp.dot(pn.astype(BF16), vn[:, hs].astype(BF16), preferred_element_type=F32))
        o = o * (1.0 / l)
        o = o[0:s_new] - lam * o[s_new:2 * s_new]
        o_ref[0, :, hs] = _rmsnorm(o, g_ref[...], SUBLN_EPS) * (1.0 - LAM_INIT)

    pending = head_scores(0)
    for hd in range(N_HEADS):
        nxt = head_scores(hd + 1) if hd + 1 < N_HEADS else None
        head_output(hd, *pending)
        pending = nxt


def _sample_bias(rel_bias, past, s_new):
    near = SAMPLE_NEAR
    j = np.arange(s_new + near - 1)
    bias_c = _toeplitz(_bias_rows(rel_bias, j - (s_new - 1) - near), s_new, near)
    j = np.arange(2 * s_new - 1)
    bias_n = _toeplitz(_bias_rows(rel_bias, j - (s_new - 1)), s_new, s_new)
    twice = lambda t: jnp.concatenate([t, t], axis=1)
    return twice(bias_c), twice(bias_n)


def _sample_attention(q, cache_k, cache_v, k_new, v_new, bias_c, bias_n, lam_vecs, subln_row):
    b, s, d = q.shape
    past = cache_k.shape[1] // N_HEADS
    far = past - SAMPLE_NEAR
    assert far >= 0 and SAMPLE_NEAR >= FAR_DISTANCE
    new_spec = pl.BlockSpec((1, s, d), lambda i: (i, 0, 0))
    cache_spec = pl.BlockSpec((1, past * N_HEADS, DV), lambda i: (i, 0, 0))
    return pl.pallas_call(
        functools.partial(_sample_attn_kernel, past=past, far=far),
        grid=(b,),
        in_specs=[new_spec, cache_spec, cache_spec, new_spec, new_spec,
                  _const_spec(bias_c.shape), _const_spec(bias_n.shape),
                  _const_spec((4, HEAD_DIM)), _const_spec((1, DV))],
        out_specs=new_spec,
        out_shape=jax.ShapeDtypeStruct((b, s, d), F32),
        compiler_params=pltpu.CompilerParams(
            dimension_semantics=("arbitrary",), vmem_limit_bytes=VMEM_LIMIT),
        name="sample_attention",
    )(q, cache_k, cache_v, k_new, v_new, bias_c, bias_n, lam_vecs, subln_row)


def _ffn_kernel(x_ref, a_ref, hist_ref, nm_ref, wug_ref, bg_ref, wp_ref, ps_ref, wo_ref,
                nf_ref, wgu_ref, wd_ref, nfin_ref, y_ref, pst_ref, carry_ref, ext_ref,
                *, streams, rows, pos0, attn_transposed):
    t = pl.program_id(1)
    n = streams * rows
    d = D_MODEL

    @pl.when(t == 0)
    def _():
        carry_ref[...] = hist_ref[...]

    x = x_ref[...].reshape(n, d)
    h = _rmsnorm(x, nm_ref[...], EPS).astype(BF16)
    u = jnp.dot(h, wug_ref[:, 0:d], preferred_element_type=F32)
    zg = jnp.dot(h, wug_ref[:, d:3 * d], preferred_element_type=F32) + bg_ref[...]

    ext_ref[:, 0:HALO, :] = carry_ref[...]
    ext_ref[:, HALO:HALO + rows, :] = u.reshape(streams, rows, d)
    last = ext_ref[:, rows:rows + HALO, :]
    carry_ref[...] = last
    pst_ref[...] = last

    pos = pos0 + t * rows + lax.broadcasted_iota(jnp.int32, (1, rows, 1), 1)
    pooled = []
    for gi, w in enumerate(POOL_WINDOWS):
        cs = slice(gi * POOL_GROUP, (gi + 1) * POOL_GROUP)
        cur = ext_ref[:, HALO:HALO + rows, cs]
        s = cur
        for back in range(1, w):
            s = s + ext_ref[:, HALO - back:HALO - back + rows, cs]
        inv = 1.0 / jnp.minimum(pos + 1, w).astype(F32)
        dlt = (s * inv - cur).reshape(n, POOL_GROUP).astype(BF16)
        pooled.append(jnp.dot(dlt, wp_ref[gi], preferred_element_type=F32))
    pool = jnp.concatenate(pooled, axis=-1) * ps_ref[...]

    g_a = jax.nn.sigmoid(zg[:, 0:d])
    g_p = jax.nn.sigmoid(zg[:, d:2 * d])
    attn = a_ref[0].T if attn_transposed else a_ref[...].reshape(n, d)
    merged = g_a * attn + g_p * pool
    x1 = x + jnp.dot(merged.astype(BF16), wo_ref[...], preferred_element_type=F32)

    h2 = _rmsnorm(x1, nf_ref[...], EPS).astype(BF16)
    gu = jnp.dot(h2, wgu_ref[...], preferred_element_type=F32)
    act = jax.nn.silu(gu[:, 0:D_FF]) * gu[:, D_FF:2 * D_FF]
    x2 = x1 + jnp.dot(act.astype(BF16), wd_ref[...], preferred_element_type=F32)
    y_ref[...] = _rmsnorm(x2, nfin_ref[...], EPS).reshape(streams, rows, d)


def _merge_ffn(x, attn, hist, weights, *, streams, rows, pos0, attn_transposed):
    b, t, d = x.shape
    tok_spec = pl.BlockSpec((streams, rows, d), lambda i, j: (i, j, 0))
    halo_spec = pl.BlockSpec((streams, HALO, d), lambda i, j: (i, 0, 0))
    if attn_transposed:
        assert streams == 1
        attn_spec = pl.BlockSpec((1, d, rows), lambda i, j: (i, 0, j))
    else:
        attn_spec = tok_spec
    w_specs = [_const_spec(w.shape) for w in weights]
    w_specs[1] = _const_spec((d, 3 * d), index=(0, 1))
    return pl.pallas_call(
        functools.partial(_ffn_kernel, streams=streams, rows=rows, pos0=pos0,
                          attn_transposed=attn_transposed),
        grid=(b // streams, t // rows),
        in_specs=[tok_spec, attn_spec, halo_spec] + w_specs,
        out_specs=[tok_spec, halo_spec],
        out_shape=[jax.ShapeDtypeStruct((b, t, d), F32),
                   jax.ShapeDtypeStruct((b, HALO, d), F32)],
        scratch_shapes=[pltpu.VMEM((streams, HALO, d), F32),
                        pltpu.VMEM((streams, HALO + rows, d), F32)],
        compiler_params=pltpu.CompilerParams(
            dimension_semantics=("arbitrary", "arbitrary"), vmem_limit_bytes=VMEM_LIMIT),
        name="merge_ffn",
    )(x, attn, hist, *weights)


def kernel(x_prompt, x_sample, cache_k, cache_v, state_pool, rel_bias, norm_mix, w_in, b_gate,
           lambda_q1, lambda_k1, lambda_q2, lambda_k2, subln_g, w_pool, pool_scale, w_out,
           norm_ffn, w_gate_up, w_down, norm_final):
    assert norm_mix.shape[0] == 1, "single layer"
    d = D_MODEL
    bp, tp, _ = x_prompt.shape
    bs, ts, _ = x_sample.shape
    past = cache_k.shape[2]
    assert tp % PROJ_ROWS == 0 and tp % FFN_ROWS == 0 and PROJ_ROWS % ATTN_BLOCK == 0
    assert ATTN_BLOCK % CHUNK == 0 and N_HEADS % ATTN_HEADS == 0 and ts >= HALO and ts % 8 == 0
    assert (past + ts - 1) // CHUNK <= past // CHUNK

    w_in_b = w_in.reshape(d, 6 * d).astype(BF16)
    ffn_weights = (norm_mix, w_in_b, b_gate, w_pool[0].astype(BF16), pool_scale,
                   w_out[0].astype(BF16), norm_ffn, w_gate_up[0].astype(BF16),
                   w_down[0].astype(BF16), norm_final.reshape(1, d))
    lam_vecs = jnp.concatenate([lambda_q1, lambda_k1, lambda_q2, lambda_k2], axis=0)

    k_p, v_p, kb, qt, vt = _project(x_prompt, norm_mix, w_in_b, rows=PROJ_ROWS, blocked=True)
    attn_p = _prompt_attention(qt, kb, vt, *_prompt_bias_tiles(rel_bias), lam_vecs,
                               subln_g.reshape(DV, 1))
    y_p, pool_p = _merge_ffn(x_prompt, attn_p, jnp.zeros((bp, HALO, d), F32), ffn_weights,
                             streams=1, rows=FFN_ROWS, pos0=0, attn_transposed=True)

    xs_flat = x_sample.reshape(1, bs * ts, d)
    q_s, k_s, v_s = _project(xs_flat, norm_mix, w_in_b, rows=bs * ts, blocked=False)
    q_s, k_s, v_s = (a.reshape(bs, ts, d) for a in (q_s, k_s, v_s))
    bias_c, bias_n = _sample_bias(rel_bias, past, ts)
    attn_s = _sample_attention(q_s, cache_k.reshape(bs, past * N_HEADS, DV),
                               cache_v.reshape(bs, past * N_HEADS, DV), k_s, v_s, bias_c, bias_n, lam_vecs, subln_g.reshape(1, DV))
    hist_s = jnp.pad(state_pool[0], ((0, 0), (HALO - POOL_HIST, 0), (0, 0)))
    y_s, pool_s = _merge_ffn(x_sample, attn_s, hist_s, ffn_weights, streams=bs, rows=ts, pos0=past,
                             attn_transposed=False)

    heads = (N_HEADS, DV)
    return (y_p, y_s,
            k_p.reshape(1, bp, tp, *heads), v_p.reshape(1, bp, tp, *heads),
            pool_p[None, :, HALO - POOL_HIST:, :],
            k_s.reshape(1, bs, ts, *heads), v_s.reshape(1, bs, ts, *heads),
            pool_s[None, :, HALO - POOL_HIST:, :])
```

```python
import functools
import math

import numpy as np
import jax
import jax.numpy as jnp
from jax import lax
from jax.experimental import pallas as pl
from jax.experimental.pallas import tpu as pltpu

F32 = jnp.float32
BF16 = jnp.bfloat16

D_MODEL = 1024
CHUNK = 64
HEAD_DIM = 64
DV = 2 * HEAD_DIM
N_HEADS = D_MODEL // DV
POOL_WINDOWS = (2, 4, 8, 16)
POOL_GROUP = D_MODEL // len(POOL_WINDOWS)
POOL_HIST = max(POOL_WINDOWS) - 1
HALO = POOL_HIST + 1
D_FF = ((8 * D_MODEL) // 3 + 255) // 256 * 256
NUM_BUCKETS = 32
MAX_DISTANCE = 128
EPS = 1e-6
SUBLN_EPS = 1e-5
LAM_INIT = 0.8 - 0.6 * math.exp(-0.3 * 0)
LOG2E = math.log2(math.e)
QK_SCALE_LOG2 = HEAD_DIM ** -0.5 * LOG2E
MASKED = -1e30
SUM_LIMIT = 2.0 ** 16

ATTN_BLOCK = 512
ATTN_HEADS = 4
PROJ_ROWS = 1024
FFN_ROWS = 512
FAR_GROUP = 4
BIAS_CORNER = 128
SAMPLE_NEAR = 128
VMEM_LIMIT = 56 * 1024 * 1024
SUBLANES = 8


def _rmsnorm(x, g, eps):
    return x * lax.rsqrt(jnp.mean(x * x, axis=-1, keepdims=True) + eps) * g


def _const_spec(shape, index=None):
    index = (0,) * len(shape) if index is None else index
    return pl.BlockSpec(shape, lambda *_: index, pipeline_mode=pl.Buffered(1))


def _proj_kernel(x_ref, g_ref, w_ref, *out_refs, rows, blocked):
    x = x_ref[0]
    h = _rmsnorm(x, g_ref[...], EPS).astype(BF16)
    q = jnp.dot(h, w_ref[:, 0:D_MODEL], preferred_element_type=F32) * QK_SCALE_LOG2
    k = jnp.dot(h, w_ref[:, D_MODEL:2 * D_MODEL], preferred_element_type=F32)
    v = jnp.dot(h, w_ref[:, 2 * D_MODEL:3 * D_MODEL], preferred_element_type=F32)
    if not blocked:
        q_ref, k_ref, v_ref = out_refs
        q_ref[0] = q
        k_ref[0] = k
        v_ref[0] = v
        return
    k_ref, v_ref, kb_ref, qt_ref, vt_ref = out_refs
    qt = q.T
    vt = v.T
    for hd in range(N_HEADS):
        hs = slice(hd * DV, (hd + 1) * DV)
        k_ref[0, pl.ds(hd, rows, stride=N_HEADS), :] = k[:, hs]
        v_ref[0, pl.ds(hd, rows, stride=N_HEADS), :] = v[:, hs]
        for c in range(rows // ATTN_BLOCK):
            cs = slice(c * ATTN_BLOCK, (c + 1) * ATTN_BLOCK)
            kb_ref[0, hd, c] = k[cs, hs].astype(BF16)
            qt_ref[0, hd, c] = qt[hs, cs].astype(BF16)
            vt_ref[0, hd, c] = vt[hs, cs].astype(BF16)


def _project(x, norm_mix, w_in_b, *, rows, blocked):
    b, t, d = x.shape
    nb_step = rows // ATTN_BLOCK
    nat = jax.ShapeDtypeStruct((b, t, d), F32)
    nat_spec = pl.BlockSpec((1, rows, d), lambda i, j: (i, j, 0))
    if blocked:
        nb = t // ATTN_BLOCK
        per_head = jax.ShapeDtypeStruct((b, t * N_HEADS, DV), F32)
        per_head_spec = pl.BlockSpec((1, rows * N_HEADS, DV), lambda i, j: (i, j, 0))
        out_shape = [per_head, per_head,
                     jax.ShapeDtypeStruct((b, N_HEADS, nb, ATTN_BLOCK, DV), BF16),
                     jax.ShapeDtypeStruct((b, N_HEADS, nb, DV, ATTN_BLOCK), BF16),
                     jax.ShapeDtypeStruct((b, N_HEADS, nb, DV, ATTN_BLOCK), BF16)]
        out_specs = [per_head_spec, per_head_spec,
                     pl.BlockSpec((1, N_HEADS, nb_step, ATTN_BLOCK, DV), lambda i, j: (i, 0, j, 0, 0)),
                     pl.BlockSpec((1, N_HEADS, nb_step, DV, ATTN_BLOCK), lambda i, j: (i, 0, j, 0, 0)),
                     pl.BlockSpec((1, N_HEADS, nb_step, DV, ATTN_BLOCK), lambda i, j: (i, 0, j, 0, 0))]
    else:
        out_shape = [nat, nat, nat]
        out_specs = [nat_spec, nat_spec, nat_spec]
    return pl.pallas_call(
        functools.partial(_proj_kernel, rows=rows, blocked=blocked),
        grid=(b, t // rows),
        in_specs=[nat_spec, _const_spec((1, d)), _const_spec((d, 3 * d))],
        out_specs=out_specs,
        out_shape=out_shape,
        compiler_params=pltpu.CompilerParams(
            dimension_semantics=("arbitrary", "arbitrary"), vmem_limit_bytes=VMEM_LIMIT),
        name="qkv_proj_blocked" if blocked else "qkv_proj",
    )(x, norm_mix, w_in_b)


def _rel_bucket_np(rel):
    nb = NUM_BUCKETS // 2
    ret = np.where(rel > 0, nb, 0)
    n = np.abs(rel)
    max_exact = nb // 2
    large = max_exact + (np.log(np.maximum(n, 1).astype(np.float64) / max_exact)
                         / math.log(MAX_DISTANCE / max_exact) * (nb - max_exact)).astype(np.int64)
    large = np.minimum(large, nb - 1)
    return (ret + np.where(n < max_exact, n, large)).astype(np.int32)


FAR_BUCKET = int(_rel_bucket_np(np.array([-MAX_DISTANCE]))[0])
FAR_DISTANCE = int(np.min(np.nonzero(_rel_bucket_np(-np.arange(4 * MAX_DISTANCE)) == FAR_BUCKET)[0]))
assert np.all(_rel_bucket_np(-np.arange(FAR_DISTANCE, 1 << 16)) == FAR_BUCKET)


def _toeplitz(vec, rows, cols):
    h, length = vec.shape
    assert length == rows + cols - 1
    ext = jnp.concatenate([vec, jnp.zeros((h, 1), vec.dtype)], axis=1)
    skew = jnp.tile(ext, (1, rows))[:, :rows * length].reshape(h, rows, length)
    return skew[:, :, rows - 1:rows - 1 + cols]


def _bias_rows(rel_bias, rels):
    table = (rel_bias - rel_bias[FAR_BUCKET][None, :]).T * LOG2E
    return jnp.take(table, jnp.asarray(_rel_bucket_np(np.asarray(rels))), axis=1)


def _lambda_from(lv):
    s1 = jnp.sum(lv[0:1] * lv[1:2], axis=-1, keepdims=True)
    s2 = jnp.sum(lv[2:3] * lv[3:4], axis=-1, keepdims=True)
    return jnp.exp(s1) - jnp.exp(s2) + LAM_INIT


def _flash_kernel(qt_ref, kb_ref, vt_ref, bdiag_ref, bcorner_ref, lv_ref, g_ref, o_ref,
                  w_ref, m_ref, acc_ref, l_ref, s_ref, bias_ref, *, blk, heads):
    i = pl.program_id(2)
    half = blk // 2
    cn = BIAS_CORNER

    @pl.when(i == 0)
    def _():
        for hd in range(heads):
            corner = bcorner_ref[hd]
            diag = bdiag_ref[hd]
            bias_ref[hd, 0] = jnp.zeros((blk, blk), F32)
            bias_ref[hd, 0, blk - cn:blk, 0:cn] = corner
            bias_ref[hd, 1, 0:half, 0:half] = diag
            bias_ref[hd, 1, half:blk, half:blk] = diag
            bias_ref[hd, 1, 0:half, half:blk] = jnp.zeros((half, half), F32)
            bias_ref[hd, 1, half - cn:half, half:half + cn] = corner
            bias_ref[hd, 1, half:blk, 0:half] = jnp.full((half, half), MASKED, F32)

    zeros = jnp.zeros((HEAD_DIM, blk), BF16)
    for hd in range(heads):
        qt = qt_ref[0, hd, 0]
        w_ref[hd, 0] = jnp.concatenate([qt[:HEAD_DIM], zeros], axis=0)
        w_ref[hd, 1] = jnp.concatenate([zeros, qt[HEAD_DIM:]], axis=0)

    chains = [(hd, mp) for hd in range(heads) for mp in range(2)]

    def key_sums(p):
        return jnp.sum(p.astype(F32).reshape(p.shape[0] // SUBLANES, SUBLANES, p.shape[1]), axis=0)

    def diagonal_step(lookahead):
        def diag_scores(hd, mp):
            sa = jnp.dot(kb_ref[0, hd, i, 0:half, :], w_ref[hd, mp],
                         preferred_element_type=F32) + bias_ref[hd, 1, 0:half, :]
            sb = jnp.dot(kb_ref[0, hd, i, half:blk, :], w_ref[hd, mp, :, half:blk],
                         preferred_element_type=F32) + bias_ref[hd, 1, half:blk, half:blk]
            return sa, sb

        def diag_consume(hd, mp, sa, sb):
            m_r = jnp.maximum(jnp.max(sa[:, half:blk], axis=0, keepdims=True),
                              jnp.max(sb, axis=0, keepdims=True))
            m = jnp.concatenate([jnp.max(sa[:, 0:half], axis=0, keepdims=True), m_r], axis=1)
            m_ref[hd, mp] = m
            pa = jnp.exp2((sa - m).astype(BF16))
            pb = jnp.exp2((sb - m_r).astype(BF16))
            acc_ref[hd, mp] = jnp.dot(vt_ref[0, hd, i, :, 0:half], pa, preferred_element_type=F32)
            acc_ref[hd, mp, :, half:blk] += jnp.dot(vt_ref[0, hd, i, :, half:blk], pb,
                                                    preferred_element_type=F32)
            l_ref[hd, mp] = key_sums(pa)
            l_ref[hd, mp, :, half:blk] += key_sums(pb)

        pending = diag_scores(*chains[0])
        out = None
        for n, (hd, mp) in enumerate(chains):
            if n + 1 < len(chains):
                nxt = diag_scores(*chains[n + 1])
            else:
                nxt = None
                out = lookahead()
            diag_consume(hd, mp, *pending)
            pending = nxt
        return out

    def scores(j, tile, hd, mp):
        s = jnp.dot(kb_ref[0, hd, j], w_ref[hd, mp], preferred_element_type=F32)
        if tile is not None:
            s = s + bias_ref[hd, tile]
        return s

    def consume(j, hd, mp, s):
        p = jnp.exp2((s - m_ref[hd, mp]).astype(BF16))
        acc_ref[hd, mp] += jnp.dot(vt_ref[0, hd, j], p, preferred_element_type=F32)
        l_ref[hd, mp] += key_sums(p)

    def fast_step(j, tile, pending=None, lookahead=None):
        if pending is None:
            pending = scores(j, tile, *chains[0])
        out = None
        for n, (hd, mp) in enumerate(chains):
            if n + 1 < len(chains):
                nxt = scores(j, tile, *chains[n + 1])
            else:
                nxt = None
                out = lookahead() if lookahead is not None else None
            consume(j, hd, mp, pending)
            pending = nxt
        return out

    n_far = jnp.maximum(i - 1, 0)

    def far_scores(j):
        return scores(j, None, *chains[0])

    s_ref[...] = diagonal_step(lambda: scores(jnp.maximum(i - 1, 0), 0, *chains[0]))

    @pl.when(i >= 1)
    def _():
        s_ref[...] = fast_step(i - 1, 0, pending=s_ref[...], lookahead=lambda: far_scores(0))

    def far_group(j0, count):
        pending = s_ref[...]
        for c in range(count):
            nxt_block = jnp.minimum(j0 + c + 1, n_far - 1)
            pending = fast_step(j0 + c, None, pending=pending,
                                lookahead=lambda nb=nxt_block: far_scores(nb))
        s_ref[...] = pending

    @pl.loop(0, n_far // FAR_GROUP)
    def _(jj):
        far_group(FAR_GROUP * jj, FAR_GROUP)

    done = (n_far // FAR_GROUP) * FAR_GROUP
    size = FAR_GROUP // 2
    while size >= 1:
        @pl.when(((n_far - done) & size) != 0)
        def _(done=done, size=size):
            far_group(done, size)
        done = done + ((n_far - done) & size)
        size //= 2

    def safe_recompute():
        @pl.loop(0, 2 * heads)
        def _(c):
            hd = c // 2
            mp = c % 2
            m_ref[hd, mp] = jnp.full((1, blk), MASKED, F32)
            acc_ref[hd, mp] = jnp.zeros((DV, blk), F32)
            l_ref[hd, mp] = jnp.zeros((SUBLANES, blk), F32)

            @pl.loop(0, i + 1)
            def _(j):
                s = jnp.dot(kb_ref[0, hd, j], w_ref[hd, mp], preferred_element_type=F32)
                tile = jnp.clip(j - (i - 1), 0, 1)
                s = s + jnp.where(j >= i - 1, bias_ref[hd, tile], 0.0)
                m_old = m_ref[hd, mp]
                m_new = jnp.maximum(m_old, jnp.max(s, axis=0, keepdims=True))
                alpha = jnp.exp2(m_old - m_new)
                p = jnp.exp2((s - m_new).astype(BF16))
                m_ref[hd, mp] = m_new
                pv = jnp.dot(vt_ref[0, hd, j], p, preferred_element_type=F32)
                acc_ref[hd, mp] = alpha * acc_ref[hd, mp] + pv
                l_ref[hd, mp] = alpha * l_ref[hd, mp] + key_sums(p)

    lam = _lambda_from(lv_ref[...])

    def finalize():
        poison = jnp.zeros((1, 1), F32)
        for hd in range(heads):
            l1 = jnp.sum(l_ref[hd, 0], axis=0, keepdims=True)
            l2 = jnp.sum(l_ref[hd, 1], axis=0, keepdims=True)
            o = acc_ref[hd, 0] * (1.0 / l1) - lam * (acc_ref[hd, 1] * (1.0 / l2))
            ms = jnp.mean(o * o, axis=0, keepdims=True)
            y = o * (lax.rsqrt(ms + SUBLN_EPS) * (1.0 - LAM_INIT)) * g_ref[...]
            o_ref[0, hd * DV:(hd + 1) * DV, :] = y
            flags = (ms + l1 + l2) * 0.0 + jnp.where(jnp.maximum(l1, l2) > SUM_LIMIT, 1.0, 0.0)
            poison = poison + jnp.sum(flags, axis=1, keepdims=True)
        return poison

    poison = finalize()

    @pl.when(jnp.logical_not(poison[0, 0] == 0.0))
    def _():
        safe_recompute()
        finalize()


def _prompt_bias_tiles(rel_bias):
    half = ATTN_BLOCK // 2
    cn = BIAS_CORNER
    assert cn >= FAR_DISTANCE - 1 and half >= cn and half % CHUNK == 0
    j = np.arange(2 * half - 1)
    diag = _toeplitz(_bias_rows(rel_bias, half - 1 - j), half, half)
    kk = np.arange(half)[:, None]
    qq = np.arange(half)[None, :]
    visible = jnp.asarray((kk // CHUNK) <= (qq // CHUNK))
    j = np.arange(2 * cn - 1)
    corner = _toeplitz(_bias_rows(rel_bias, -1 - j), cn, cn)
    return jnp.where(visible[None], diag, MASKED), corner


def _prompt_attention(qt, kb, vt, bias_diag, bias_corner, lam_vecs, subln_col):
    b, nh, nb, _, blk = qt.shape
    t = nb * blk
    hps = ATTN_HEADS
    return pl.pallas_call(
        functools.partial(_flash_kernel, blk=blk, heads=hps),
        grid=(b, nh // hps, nb),
        in_specs=[
            pl.BlockSpec((1, hps, 1, DV, blk), lambda bi, h, i: (bi, h, i, 0, 0)),
            pl.BlockSpec((1, hps, nb, blk, DV), lambda bi, h, i: (bi, h, 0, 0, 0)),
            pl.BlockSpec((1, hps, nb, DV, blk), lambda bi, h, i: (bi, h, 0, 0, 0)),
            pl.BlockSpec((hps, blk // 2, blk // 2), lambda bi, h, i: (h, 0, 0),
                         pipeline_mode=pl.Buffered(1)),
            pl.BlockSpec((hps, BIAS_CORNER, BIAS_CORNER), lambda bi, h, i: (h, 0, 0),
                         pipeline_mode=pl.Buffered(1)),
            _const_spec((4, HEAD_DIM)),
            _const_spec((DV, 1)),
        ],
        out_specs=pl.BlockSpec((1, hps * DV, blk), lambda bi, h, i: (bi, h, i)),
        out_shape=jax.ShapeDtypeStruct((b, nh * DV, t), F32),
        scratch_shapes=[pltpu.VMEM((hps, 2, DV, blk), BF16),
                        pltpu.VMEM((hps, 2, 1, blk), F32),
                        pltpu.VMEM((hps, 2, DV, blk), F32),
                        pltpu.VMEM((hps, 2, SUBLANES, blk), F32),
                        pltpu.VMEM((blk, blk), F32),
                        pltpu.VMEM((hps, 2, blk, blk), F32)],
        compiler_params=pltpu.CompilerParams(
            dimension_semantics=("arbitrary", "arbitrary", "arbitrary"),
            vmem_limit_bytes=VMEM_LIMIT),
        name="prompt_attention",
    )(qt, kb, vt, bias_diag, bias_corner, lam_vecs, subln_col)


def _sample_attn_kernel(q_ref, ck_ref, cv_ref, kn_ref, vn_ref, bc_ref, bn_ref, lv_ref, g_ref, o_ref,
                        *, past, far):
    lam = _lambda_from(lv_ref[...])
    q = q_ref[0]
    kn = kn_ref[0]
    vn = vn_ref[0]
    s_new = q.shape[0]
    lane = lax.broadcasted_iota(jnp.int32, (s_new, DV), 1)
    nt = (((1,), (1,)), ((), ()))

    def head_scores(hd):
        hs = slice(hd * DV, (hd + 1) * DV)
        qh = q[:, hs]
        qq = jnp.concatenate([jnp.where(lane < HEAD_DIM, qh, 0.0),
                              jnp.where(lane >= HEAD_DIM, qh, 0.0)], axis=0).astype(BF16)
        kc = ck_ref[0, pl.ds(hd, past, stride=N_HEADS), :].astype(BF16)
        sf = lax.dot_general(qq, kc[:far], nt, preferred_element_type=F32)
        sc = lax.dot_general(qq, kc[far:], nt, preferred_element_type=F32) + bc_ref[hd]
        sn = lax.dot_general(qq, kn[:, hs].astype(BF16), nt, preferred_element_type=F32) + bn_ref[hd]
        return sf, sc, sn

    def head_output(hd, sf, sc, sn):
        hs = slice(hd * DV, (hd + 1) * DV)
        vc = cv_ref[0, pl.ds(hd, past, stride=N_HEADS), :].astype(BF16)
        m = jnp.maximum(jnp.maximum(jnp.max(sf, axis=-1, keepdims=True),
                                    jnp.max(sc, axis=-1, keepdims=True)),
                        jnp.max(sn, axis=-1, keepdims=True))
        pf = jnp.exp2(sf - m)
        pc = jnp.exp2(sc - m)
        pn = jnp.exp2(sn - m)
        l = (jnp.sum(pf, axis=-1, keepdims=True) + jnp.sum(pc, axis=-1, keepdims=True)
             + jnp.sum(pn, axis=-1, keepdims=True))
        o = (jnp.dot(pf.astype(BF16), vc[:far], preferred_element_type=F32)
             + jnp.dot(pc.astype(BF16), vc[far:], preferred_element_type=F32)
             + jnp.dot(pn.astype(BF16), vn[:, hs].astype(BF16), preferred_element_type=F32))
        o = o * (1.0 / l)
        o = o[0:s_new] - lam * o[s_new:2 * s_new]
        o_ref[0, :, hs] = _rmsnorm(o, g_ref[...], SUBLN_EPS) * (1.0 - LAM_INIT)

    pending = head_scores(0)
    for hd in range(N_HEADS):
        nxt = head_scores(hd + 1) if hd + 1 < N_HEADS else None
        head_output(hd, *pending)
        pending = nxt


def _sample_bias(rel_bias, past, s_new):
    near = SAMPLE_NEAR
    j = np.arange(s_new + near - 1)
    bias_c = _toeplitz(_bias_rows(rel_bias, j - (s_new - 1) - near), s_new, near)
    j = np.arange(2 * s_new - 1)
    bias_n = _toeplitz(_bias_rows(rel_bias, j - (s_new - 1)), s_new, s_new)
    twice = lambda t: jnp.concatenate([t, t], axis=1)
    return twice(bias_c), twice(bias_n)


def _sample_attention(q, cache_k, cache_v, k_new, v_new, bias_c, bias_n, lam_vecs, subln_row):
    b, s, d = q.shape
    past = cache_k.shape[1] // N_HEADS
    far = past - SAMPLE_NEAR
    assert far >= 0 and SAMPLE_NEAR >= FAR_DISTANCE
    new_spec = pl.BlockSpec((1, s, d), lambda i: (i, 0, 0))
    cache_spec = pl.BlockSpec((1, past * N_HEADS, DV), lambda i: (i, 0, 0))
    return pl.pallas_call(
        functools.partial(_sample_attn_kernel, past=past, far=far),
        grid=(b,),
        in_specs=[new_spec, cache_spec, cache_spec, new_spec, new_spec,
                  _const_spec(bias_c.shape), _const_spec(bias_n.shape),
                  _const_spec((4, HEAD_DIM)), _const_spec((1, DV))],
        out_specs=new_spec,
        out_shape=jax.ShapeDtypeStruct((b, s, d), F32),
        compiler_params=pltpu.CompilerParams(
            dimension_semantics=("arbitrary",), vmem_limit_bytes=VMEM_LIMIT),
        name="sample_attention",
    )(q, cache_k, cache_v, k_new, v_new, bias_c, bias_n, lam_vecs, subln_row)


def _ffn_kernel(x_ref, a_ref, hist_ref, nm_ref, wug_ref, bg_ref, wp_ref, ps_ref, wo_ref,
                nf_ref, wgu_ref, wd_ref, nfin_ref, y_ref, pst_ref, carry_ref, ext_ref,
                *, streams, rows, pos0, attn_transposed):
    t = pl.program_id(1)
    n = streams * rows
    d = D_MODEL

    @pl.when(t == 0)
    def _():
        carry_ref[...] = hist_ref[...]

    x = x_ref[...].reshape(n, d)
    h = _rmsnorm(x, nm_ref[...], EPS).astype(BF16)
    u = jnp.dot(h, wug_ref[:, 0:d], preferred_element_type=F32)
    zg = jnp.dot(h, wug_ref[:, d:3 * d], preferred_element_type=F32) + bg_ref[...]

    ext_ref[:, 0:HALO, :] = carry_ref[...]
    ext_ref[:, HALO:HALO + rows, :] = u.reshape(streams, rows, d)
    last = ext_ref[:, rows:rows + HALO, :]
    carry_ref[...] = last
    pst_ref[...] = last

    pos = pos0 + t * rows + lax.broadcasted_iota(jnp.int32, (1, rows, 1), 1)
    pooled = []
    for gi, w in enumerate(POOL_WINDOWS):
        cs = slice(gi * POOL_GROUP, (gi + 1) * POOL_GROUP)
        cur = ext_ref[:, HALO:HALO + rows, cs]
        s = cur
        for back in range(1, w):
            s = s + ext_ref[:, HALO - back:HALO - back + rows, cs]
        inv = 1.0 / jnp.minimum(pos + 1, w).astype(F32)
        dlt = (s * inv - cur).reshape(n, POOL_GROUP).astype(BF16)
        pooled.append(jnp.dot(dlt, wp_ref[gi], preferred_element_type=F32))
    pool = jnp.concatenate(pooled, axis=-1) * ps_ref[...]

    g_a = jax.nn.sigmoid(zg[:, 0:d])
    g_p = jax.nn.sigmoid(zg[:, d:2 * d])
    attn = a_ref[0].T if attn_transposed else a_ref[...].reshape(n, d)
    merged = g_a * attn + g_p * pool
    x1 = x + jnp.dot(merged.astype(BF16), wo_ref[...], preferred_element_type=F32)

    h2 = _rmsnorm(x1, nf_ref[...], EPS).astype(BF16)
    gu = jnp.dot(h2, wgu_ref[...], preferred_element_type=F32)
    act = jax.nn.silu(gu[:, 0:D_FF]) * gu[:, D_FF:2 * D_FF]
    x2 = x1 + jnp.dot(act.astype(BF16), wd_ref[...], preferred_element_type=F32)
    y_ref[...] = _rmsnorm(x2, nfin_ref[...], EPS).reshape(streams, rows, d)


def _merge_ffn(x, attn, hist, weights, *, streams, rows, pos0, attn_transposed):
    b, t, d = x.shape
    tok_spec = pl.BlockSpec((streams, rows, d), lambda i, j: (i, j, 0))
    halo_spec = pl.BlockSpec((streams, HALO, d), lambda i, j: (i, 0, 0))
    if attn_transposed:
        assert streams == 1
        attn_spec = pl.BlockSpec((1, d, rows), lambda i, j: (i, 0, j))
    else:
        attn_spec = tok_spec
    w_specs = [_const_spec(w.shape) for w in weights]
    w_specs[1] = _const_spec((d, 3 * d), index=(0, 1))
    return pl.pallas_call(
        functools.partial(_ffn_kernel, streams=streams, rows=rows, pos0=pos0,
                          attn_transposed=attn_transposed),
        grid=(b // streams, t // rows),
        in_specs=[tok_spec, attn_spec, halo_spec] + w_specs,
        out_specs=[tok_spec, halo_spec],
        out_shape=[jax.ShapeDtypeStruct((b, t, d), F32),
                   jax.ShapeDtypeStruct((b, HALO, d), F32)],
        scratch_shapes=[pltpu.VMEM((streams, HALO, d), F32),
                        pltpu.VMEM((streams, HALO + rows, d), F32)],
        compiler_params=pltpu.CompilerParams(
            dimension_semantics=("arbitrary", "arbitrary"), vmem_limit_bytes=VMEM_LIMIT),
        name="merge_ffn",
    )(x, attn, hist, *weights)


def kernel(x_prompt, x_sample, cache_k, cache_v, state_pool, rel_bias, norm_mix, w_in, b_gate,
           lambda_q1, lambda_k1, lambda_q2, lambda_k2, subln_g, w_pool, pool_scale, w_out,
           norm_ffn, w_gate_up, w_down, norm_final):
    assert norm_mix.shape[0] == 1, "single layer"
    d = D_MODEL
    bp, tp, _ = x_prompt.shape
    bs, ts, _ = x_sample.shape
    past = cache_k.shape[2]
    assert tp % PROJ_ROWS == 0 and tp % FFN_ROWS == 0 and PROJ_ROWS % ATTN_BLOCK == 0
    assert ATTN_BLOCK % CHUNK == 0 and N_HEADS % ATTN_HEADS == 0 and ts >= HALO and ts % SUBLANES == 0
    assert (past + ts - 1) // CHUNK <= past // CHUNK

    w_in_b = w_in.reshape(d, 6 * d).astype(BF16)
    ffn_weights = (norm_mix, w_in_b, b_gate, w_pool[0].astype(BF16), pool_scale,
                   w_out[0].astype(BF16), norm_ffn, w_gate_up[0].astype(BF16),
                   w_down[0].astype(BF16), norm_final.reshape(1, d))
    lam_vecs = jnp.concatenate([lambda_q1, lambda_k1, lambda_q2, lambda_k2], axis=0)

    k_p, v_p, kb, qt, vt = _project(x_prompt, norm_mix, w_in_b, rows=PROJ_ROWS, blocked=True)
    attn_p = _prompt_attention(qt, kb, vt, *_prompt_bias_tiles(rel_bias), lam_vecs,
                               subln_g.reshape(DV, 1))
    y_p, pool_p = _merge_ffn(x_prompt, attn_p, jnp.zeros((bp, HALO, d), F32), ffn_weights,
                             streams=1, rows=FFN_ROWS, pos0=0, attn_transposed=True)

    xs_flat = x_sample.reshape(1, bs * ts, d)
    q_s, k_s, v_s = _project(xs_flat, norm_mix, w_in_b, rows=bs * ts, blocked=False)
    q_s, k_s, v_s = (a.reshape(bs, ts, d) for a in (q_s, k_s, v_s))
    bias_c, bias_n = _sample_bias(rel_bias, past, ts)
    attn_s = _sample_attention(q_s, cache_k.reshape(bs, past * N_HEADS, DV),
                               cache_v.reshape(bs, past * N_HEADS, DV), k_s, v_s, bias_c, bias_n,
                               lam_vecs, subln_g.reshape(1, DV))
    hist_s = jnp.pad(state_pool[0], ((0, 0), (HALO - POOL_HIST, 0), (0, 0)))
    y_s, pool_s = _merge_ffn(x_sample, attn_s, hist_s, ffn_weights, streams=bs, rows=ts, pos0=past,
                             attn_transposed=False)

    heads = (N_HEADS, DV)
    return (y_p, y_s,
            k_p.reshape(1, bp, tp, *heads), v_p.reshape(1, bp, tp, *heads),
            pool_p[None, :, HALO - POOL_HIST:, :],
            k_s.reshape(1, bs, ts, *heads), v_s.reshape(1, bs, ts, *heads),
            pool_s[None, :, HALO - POOL_HIST:, :])
```

```python
import functools
import math

import numpy as np
import jax
import jax.numpy as jnp
from jax import lax
from jax.experimental import pallas as pl
from jax.experimental.pallas import tpu as pltpu

F32 = jnp.float32
BF16 = jnp.bfloat16

D_MODEL = 1024
CHUNK = 64
HEAD_DIM = 64
DV = 2 * HEAD_DIM
N_HEADS = D_MODEL // DV
POOL_WINDOWS = (2, 4, 8, 16)
POOL_GROUP = D_MODEL // len(POOL_WINDOWS)
POOL_HIST = max(POOL_WINDOWS) - 1
HALO = POOL_HIST + 1
D_FF = ((8 * D_MODEL) // 3 + 255) // 256 * 256
NUM_BUCKETS = 32
MAX_DISTANCE = 128
EPS = 1e-6
SUBLN_EPS = 1e-5
LAM_INIT = 0.8 - 0.6 * math.exp(-0.3 * 0)
LOG2E = math.log2(math.e)
QK_SCALE_LOG2 = HEAD_DIM ** -0.5 * LOG2E
MASKED = -1e30
SUM_LIMIT = 2.0 ** 16

ATTN_BLOCK = 512
ATTN_HEADS = 4
PROJ_ROWS = 1024
FFN_ROWS = 512
FAR_GROUP = 4
BIAS_CORNER = 128
SAMPLE_NEAR = 128
VMEM_LIMIT = 56 * 1024 * 1024
SUBLANES = 8


def _rmsnorm(x, g, eps):
    return x * lax.rsqrt(jnp.mean(x * x, axis=-1, keepdims=True) + eps) * g


def _const_spec(shape, index=None):
    index = (0,) * len(shape) if index is None else index
    return pl.BlockSpec(shape, lambda *_: index, pipeline_mode=pl.Buffered(1))


def _proj_kernel(x_ref, g_ref, w_ref, *out_refs, rows, blocked):
    x = x_ref[0]
    h = _rmsnorm(x, g_ref[...], EPS).astype(BF16)
    q = jnp.dot(h, w_ref[:, 0:D_MODEL], preferred_element_type=F32) * QK_SCALE_LOG2
    k = jnp.dot(h, w_ref[:, D_MODEL:2 * D_MODEL], preferred_element_type=F32)
    v = jnp.dot(h, w_ref[:, 2 * D_MODEL:3 * D_MODEL], preferred_element_type=F32)
    if not blocked:
        q_ref, k_ref, v_ref = out_refs
        q_ref[0] = q
        k_ref[0] = k
        v_ref[0] = v
        return
    k_ref, v_ref, kb_ref, qt_ref, vt_ref = out_refs
    qt = q.T
    vt = v.T
    for hd in range(N_HEADS):
        hs = slice(hd * DV, (hd + 1) * DV)
        k_ref[0, pl.ds(hd, rows, stride=N_HEADS), :] = k[:, hs]
        v_ref[0, pl.ds(hd, rows, stride=N_HEADS), :] = v[:, hs]
        for c in range(rows // ATTN_BLOCK):
            cs = slice(c * ATTN_BLOCK, (c + 1) * ATTN_BLOCK)
            kb_ref[0, hd, c] = k[cs, hs].astype(BF16)
            qt_ref[0, hd, c] = qt[hs, cs].astype(BF16)
            vt_ref[0, hd, c] = vt[hs, cs].astype(BF16)


def _project(x, norm_mix, w_in_b, *, rows, blocked):
    b, t, d = x.shape
    nb_step = rows // ATTN_BLOCK
    nat = jax.ShapeDtypeStruct((b, t, d), F32)
    nat_spec = pl.BlockSpec((1, rows, d), lambda i, j: (i, j, 0))
    if blocked:
        nb = t // ATTN_BLOCK
        per_head = jax.ShapeDtypeStruct((b, t * N_HEADS, DV), F32)
        per_head_spec = pl.BlockSpec((1, rows * N_HEADS, DV), lambda i, j: (i, j, 0))
        out_shape = [per_head, per_head,
                     jax.ShapeDtypeStruct((b, N_HEADS, nb, ATTN_BLOCK, DV), BF16),
                     jax.ShapeDtypeStruct((b, N_HEADS, nb, DV, ATTN_BLOCK), BF16),
                     jax.ShapeDtypeStruct((b, N_HEADS, nb, DV, ATTN_BLOCK), BF16)]
        out_specs = [per_head_spec, per_head_spec,
                     pl.BlockSpec((1, N_HEADS, nb_step, ATTN_BLOCK, DV), lambda i, j: (i, 0, j, 0, 0)),
                     pl.BlockSpec((1, N_HEADS, nb_step, DV, ATTN_BLOCK), lambda i, j: (i, 0, j, 0, 0)),
                     pl.BlockSpec((1, N_HEADS, nb_step, DV, ATTN_BLOCK), lambda i, j: (i, 0, j, 0, 0))]
    else:
        out_shape = [nat, nat, nat]
        out_specs = [nat_spec, nat_spec, nat_spec]
    return pl.pallas_call(
        functools.partial(_proj_kernel, rows=rows, blocked=blocked),
        grid=(b, t // rows),
        in_specs=[nat_spec, _const_spec((1, d)), _const_spec((d, 3 * d))],
        out_specs=out_specs,
        out_shape=out_shape,
        compiler_params=pltpu.CompilerParams(
            dimension_semantics=("arbitrary", "arbitrary"), vmem_limit_bytes=VMEM_LIMIT),
        name="qkv_proj_blocked" if blocked else "qkv_proj",
    )(x, norm_mix, w_in_b)


def _rel_bucket_np(rel):
    nb = NUM_BUCKETS // 2
    ret = np.where(rel > 0, nb, 0)
    n = np.abs(rel)
    max_exact = nb // 2
    large = max_exact + (np.log(np.maximum(n, 1).astype(np.float64) / max_exact)
                         / math.log(MAX_DISTANCE / max_exact) * (nb - max_exact)).astype(np.int64)
    large = np.minimum(large, nb - 1)
    return (ret + np.where(n < max_exact, n, large)).astype(np.int32)


FAR_BUCKET = int(_rel_bucket_np(np.array([-MAX_DISTANCE]))[0])
FAR_DISTANCE = int(np.min(np.nonzero(_rel_bucket_np(-np.arange(4 * MAX_DISTANCE)) == FAR_BUCKET)[0]))
assert np.all(_rel_bucket_np(-np.arange(FAR_DISTANCE, 1 << 16)) == FAR_BUCKET)


def _toeplitz(vec, rows, cols):
    h, length = vec.shape
    assert length == rows + cols - 1
    ext = jnp.concatenate([vec, jnp.zeros((h, 1), vec.dtype)], axis=1)
    skew = jnp.tile(ext, (1, rows))[:, :rows * length].reshape(h, rows, length)
    return skew[:, :, rows - 1:rows - 1 + cols]


def _bias_rows(rel_bias, rels):
    table = (rel_bias - rel_bias[FAR_BUCKET][None, :]).T * LOG2E
    return jnp.take(table, jnp.asarray(_rel_bucket_np(np.asarray(rels))), axis=1)


def _lambda_from(lv):
    s1 = jnp.sum(lv[0:1] * lv[1:2], axis=-1, keepdims=True)
    s2 = jnp.sum(lv[2:3] * lv[3:4], axis=-1, keepdims=True)
    return jnp.exp(s1) - jnp.exp(s2) + LAM_INIT


def _flash_kernel(qt_ref, kb_ref, vt_ref, bdiag_ref, bcorner_ref, lv_ref, g_ref, o_ref,
                  w_ref, m_ref, acc_ref, l_ref, s_ref, bias_ref, *, blk, heads):
    i = pl.program_id(2)
    half = blk // 2
    cn = BIAS_CORNER

    @pl.when(i == 0)
    def _():
        for hd in range(heads):
            corner = bcorner_ref[hd]
            diag = bdiag_ref[hd]
            bias_ref[hd, 0] = jnp.zeros((blk, blk), F32)
            bias_ref[hd, 0, blk - cn:blk, 0:cn] = corner
            bias_ref[hd, 1, 0:half, 0:half] = diag
            bias_ref[hd, 1, half:blk, half:blk] = diag
            bias_ref[hd, 1, 0:half, half:blk] = jnp.zeros((half, half), F32)
            bias_ref[hd, 1, half - cn:half, half:half + cn] = corner
            bias_ref[hd, 1, half:blk, 0:half] = jnp.full((half, half), MASKED, F32)

    zeros = jnp.zeros((HEAD_DIM, blk), BF16)
    for hd in range(heads):
        qt = qt_ref[0, hd, 0]
        w_ref[hd, 0] = jnp.concatenate([qt[:HEAD_DIM], zeros], axis=0)
        w_ref[hd, 1] = jnp.concatenate([zeros, qt[HEAD_DIM:]], axis=0)

    chains = [(hd, mp) for hd in range(heads) for mp in range(2)]

    def key_sums(p):
        return jnp.sum(p.astype(F32).reshape(p.shape[0] // SUBLANES, SUBLANES, p.shape[1]), axis=0)

    def diagonal_step(lookahead):
        def diag_scores(hd, mp):
            sa = jnp.dot(kb_ref[0, hd, i, 0:half, :], w_ref[hd, mp],
                         preferred_element_type=F32) + bias_ref[hd, 1, 0:half, :]
            sb = jnp.dot(kb_ref[0, hd, i, half:blk, :], w_ref[hd, mp, :, half:blk],
                         preferred_element_type=F32) + bias_ref[hd, 1, half:blk, half:blk]
            return sa, sb

        def diag_consume(hd, mp, sa, sb):
            m_r = jnp.maximum(jnp.max(sa[:, half:blk], axis=0, keepdims=True),
                              jnp.max(sb, axis=0, keepdims=True))
            m = jnp.concatenate([jnp.max(sa[:, 0:half], axis=0, keepdims=True), m_r], axis=1)
            m_ref[hd, mp] = m
            pa = jnp.exp2(sa - m)
            pb = jnp.exp2(sb - m_r)
            acc_ref[hd, mp] = jnp.dot(vt_ref[0, hd, i, :, 0:half], pa.astype(BF16),
                                      preferred_element_type=F32)
            acc_ref[hd, mp, :, half:blk] += jnp.dot(vt_ref[0, hd, i, :, half:blk], pb.astype(BF16),
                                                    preferred_element_type=F32)
            l_ref[hd, mp] = key_sums(pa)
            l_ref[hd, mp, :, half:blk] += key_sums(pb)

        pending = diag_scores(*chains[0])
        out = None
        for n, (hd, mp) in enumerate(chains):
            if n + 1 < len(chains):
                nxt = diag_scores(*chains[n + 1])
            else:
                nxt = None
                out = lookahead()
            diag_consume(hd, mp, *pending)
            pending = nxt
        return out

    def scores(j, tile, hd, mp):
        s = jnp.dot(kb_ref[0, hd, j], w_ref[hd, mp], preferred_element_type=F32)
        if tile is not None:
            s = s + bias_ref[hd, tile]
        return s

    def consume(j, hd, mp, s):
        p = jnp.exp2((s - m_ref[hd, mp]).astype(BF16))
        acc_ref[hd, mp] += jnp.dot(vt_ref[0, hd, j], p, preferred_element_type=F32)
        l_ref[hd, mp] += key_sums(p)

    def fast_step(j, tile, pending=None, lookahead=None):
        if pending is None:
            pending = scores(j, tile, *chains[0])
        out = None
        for n, (hd, mp) in enumerate(chains):
            if n + 1 < len(chains):
                nxt = scores(j, tile, *chains[n + 1])
            else:
                nxt = None
                out = lookahead() if lookahead is not None else None
            consume(j, hd, mp, pending)
            pending = nxt
        return out

    n_far = jnp.maximum(i - 1, 0)

    def far_scores(j):
        return scores(j, None, *chains[0])

    s_ref[...] = diagonal_step(lambda: scores(jnp.maximum(i - 1, 0), 0, *chains[0]))

    @pl.when(i >= 1)
    def _():
        s_ref[...] = fast_step(i - 1, 0, pending=s_ref[...], lookahead=lambda: far_scores(0))

    def far_group(j0, count):
        pending = s_ref[...]
        for c in range(count):
            nxt_block = jnp.minimum(j0 + c + 1, n_far - 1)
            pending = fast_step(j0 + c, None, pending=pending,
                                lookahead=lambda nb=nxt_block: far_scores(nb))
        s_ref[...] = pending

    @pl.loop(0, n_far // FAR_GROUP)
    def _(jj):
        far_group(FAR_GROUP * jj, FAR_GROUP)

    done = (n_far // FAR_GROUP) * FAR_GROUP
    size = FAR_GROUP // 2
    while size >= 1:
        @pl.when(((n_far - done) & size) != 0)
        def _(done=done, size=size):
            far_group(done, size)
        done = done + ((n_far - done) & size)
        size //= 2

    def safe_recompute():
        @pl.loop(0, 2 * heads)
        def _(c):
            hd = c // 2
            mp = c % 2
            m_ref[hd, mp] = jnp.full((1, blk), MASKED, F32)
            acc_ref[hd, mp] = jnp.zeros((DV, blk), F32)
            l_ref[hd, mp] = jnp.zeros((SUBLANES, blk), F32)

            @pl.loop(0, i + 1)
            def _(j):
                s = jnp.dot(kb_ref[0, hd, j], w_ref[hd, mp], preferred_element_type=F32)
                tile = jnp.clip(j - (i - 1), 0, 1)
                s = s + jnp.where(j >= i - 1, bias_ref[hd, tile], 0.0)
                m_old = m_ref[hd, mp]
                m_new = jnp.maximum(m_old, jnp.max(s, axis=0, keepdims=True))
                alpha = jnp.exp2(m_old - m_new)
                p = jnp.exp2((s - m_new).astype(BF16))
                m_ref[hd, mp] = m_new
                pv = jnp.dot(vt_ref[0, hd, j], p, preferred_element_type=F32)
                acc_ref[hd, mp] = alpha * acc_ref[hd, mp] + pv
                l_ref[hd, mp] = alpha * l_ref[hd, mp] + key_sums(p)

    lam = _lambda_from(lv_ref[...])

    def finalize():
        poison = jnp.zeros((1, 1), F32)
        for hd in range(heads):
            l1 = jnp.sum(l_ref[hd, 0], axis=0, keepdims=True)
            l2 = jnp.sum(l_ref[hd, 1], axis=0, keepdims=True)
            o = acc_ref[hd, 0] * (1.0 / l1) - lam * (acc_ref[hd, 1] * (1.0 / l2))
            ms = jnp.mean(o * o, axis=0, keepdims=True)
            y = o * (lax.rsqrt(ms + SUBLN_EPS) * (1.0 - LAM_INIT)) * g_ref[...]
            o_ref[0, hd * DV:(hd + 1) * DV, :] = y
            flags = (ms + l1 + l2) * 0.0 + jnp.where(jnp.maximum(l1, l2) > SUM_LIMIT, 1.0, 0.0)
            poison = poison + jnp.sum(flags, axis=1, keepdims=True)
        return poison

    poison = finalize()

    @pl.when(jnp.logical_not(poison[0, 0] == 0.0))
    def _():
        safe_recompute()
        finalize()


def _prompt_bias_tiles(rel_bias):
    half = ATTN_BLOCK // 2
    cn = BIAS_CORNER
    assert cn >= FAR_DISTANCE - 1 and half >= cn and half % CHUNK == 0
    j = np.arange(2 * half - 1)
    diag = _toeplitz(_bias_rows(rel_bias, half - 1 - j), half, half)
    kk = np.arange(half)[:, None]
    qq = np.arange(half)[None, :]
    visible = jnp.asarray((kk // CHUNK) <= (qq // CHUNK))
    j = np.arange(2 * cn - 1)
    corner = _toeplitz(_bias_rows(rel_bias, -1 - j), cn, cn)
    return jnp.where(visible[None], diag, MASKED), corner


def _prompt_attention(qt, kb, vt, bias_diag, bias_corner, lam_vecs, subln_col):
    b, nh, nb, _, blk = qt.shape
    t = nb * blk
    hps = ATTN_HEADS
    return pl.pallas_call(
        functools.partial(_flash_kernel, blk=blk, heads=hps),
        grid=(b, nh // hps, nb),
        in_specs=[
            pl.BlockSpec((1, hps, 1, DV, blk), lambda bi, h, i: (bi, h, i, 0, 0)),
            pl.BlockSpec((1, hps, nb, blk, DV), lambda bi, h, i: (bi, h, 0, 0, 0)),
            pl.BlockSpec((1, hps, nb, DV, blk), lambda bi, h, i: (bi, h, 0, 0, 0)),
            pl.BlockSpec((hps, blk // 2, blk // 2), lambda bi, h, i: (h, 0, 0),
                         pipeline_mode=pl.Buffered(1)),
            pl.BlockSpec((hps, BIAS_CORNER, BIAS_CORNER), lambda bi, h, i: (h, 0, 0),
                         pipeline_mode=pl.Buffered(1)),
            _const_spec((4, HEAD_DIM)),
            _const_spec((DV, 1)),
        ],
        out_specs=pl.BlockSpec((1, hps * DV, blk), lambda bi, h, i: (bi, h, i)),
        out_shape=jax.ShapeDtypeStruct((b, nh * DV, t), F32),
        scratch_shapes=[pltpu.VMEM((hps, 2, DV, blk), BF16),
                        pltpu.VMEM((hps, 2, 1, blk), F32),
                        pltpu.VMEM((hps, 2, DV, blk), F32),
                        pltpu.VMEM((hps, 2, SUBLANES, blk), F32),
                        pltpu.VMEM((blk, blk), F32),
                        pltpu.VMEM((hps, 2, blk, blk), F32)],
        compiler_params=pltpu.CompilerParams(
            dimension_semantics=("arbitrary", "arbitrary", "arbitrary"),
            vmem_limit_bytes=VMEM_LIMIT),
        name="prompt_attention",
    )(qt, kb, vt, bias_diag, bias_corner, lam_vecs, subln_col)


def _sample_attn_kernel(q_ref, ck_ref, cv_ref, kn_ref, vn_ref, bc_ref, bn_ref, lv_ref, g_ref, o_ref,
                        *, past, far):
    lam = _lambda_from(lv_ref[...])
    q = q_ref[0]
    kn = kn_ref[0]
    vn = vn_ref[0]
    s_new = q.shape[0]
    lane = lax.broadcasted_iota(jnp.int32, (s_new, DV), 1)
    nt = (((1,), (1,)), ((), ()))

    def head_scores(hd):
        hs = slice(hd * DV, (hd + 1) * DV)
        qh = q[:, hs]
        qq = jnp.concatenate([jnp.where(lane < HEAD_DIM, qh, 0.0),
                              jnp.where(lane >= HEAD_DIM, qh, 0.0)], axis=0).astype(BF16)
        kc = ck_ref[0, pl.ds(hd, past, stride=N_HEADS), :].astype(BF16)
        sf = lax.dot_general(qq, kc[:far], nt, preferred_element_type=F32)
        sc = lax.dot_general(qq, kc[far:], nt, preferred_element_type=F32) + bc_ref[hd]
        sn = lax.dot_general(qq, kn[:, hs].astype(BF16), nt, preferred_element_type=F32) + bn_ref[hd]
        return sf, sc, sn

    def head_output(hd, sf, sc, sn):
        hs = slice(hd * DV, (hd + 1) * DV)
        vc = cv_ref[0, pl.ds(hd, past, stride=N_HEADS), :].astype(BF16)
        m = jnp.maximum(jnp.maximum(jnp.max(sf, axis=-1, keepdims=True),
                                    jnp.max(sc, axis=-1, keepdims=True)),
                        jnp.max(sn, axis=-1, keepdims=True))
        pf = jnp.exp2(sf - m)
        pc = jnp.exp2(sc - m)
        pn = jnp.exp2(sn - m)
        l = (jnp.sum(pf, axis=-1, keepdims=True) + jnp.sum(pc, axis=-1, keepdims=True)
             + jnp.sum(pn, axis=-1, keepdims=True))
        o = (jnp.dot(pf.astype(BF16), vc[:far], preferred_element_type=F32)
             + jnp.dot(pc.astype(BF16), vc[far:], preferred_element_type=F32)
             + jnp.dot(pn.astype(BF16), vn[:, hs].astype(BF16), preferred_element_type=F32))
        o = o * (1.0 / l)
        o = o[0:s_new] - lam * o[s_new:2 * s_new]
        o_ref[0, :, hs] = _rmsnorm(o, g_ref[...], SUBLN_EPS) * (1.0 - LAM_INIT)

    pending = head_scores(0)
    for hd in range(N_HEADS):
        nxt = head_scores(hd + 1) if hd + 1 < N_HEADS else None
        head_output(hd, *pending)
        pending = nxt


def _sample_bias(rel_bias, past, s_new):
    near = SAMPLE_NEAR
    j = np.arange(s_new + near - 1)
    bias_c = _toeplitz(_bias_rows(rel_bias, j - (s_new - 1) - near), s_new, near)
    j = np.arange(2 * s_new - 1)
    bias_n = _toeplitz(_bias_rows(rel_bias, j - (s_new - 1)), s_new, s_new)
    twice = lambda t: jnp.concatenate([t, t], axis=1)
    return twice(bias_c), twice(bias_n)


def _sample_attention(q, cache_k, cache_v, k_new, v_new, bias_c, bias_n, lam_vecs, subln_row):
    b, s, d = q.shape
    past = cache_k.shape[1] // N_HEADS
    far = past - SAMPLE_NEAR
    assert far >= 0 and SAMPLE_NEAR >= FAR_DISTANCE
    new_spec = pl.BlockSpec((1, s, d), lambda i: (i, 0, 0))
    cache_spec = pl.BlockSpec((1, past * N_HEADS, DV), lambda i: (i, 0, 0))
    return pl.pallas_call(
        functools.partial(_sample_attn_kernel, past=past, far=far),
        grid=(b,),
        in_specs=[new_spec, cache_spec, cache_spec, new_spec, new_spec,
                  _const_spec(bias_c.shape), _const_spec(bias_n.shape),
                  _const_spec((4, HEAD_DIM)), _const_spec((1, DV))],
        out_specs=new_spec,
        out_shape=jax.ShapeDtypeStruct((b, s, d), F32),
        compiler_params=pltpu.CompilerParams(
            dimension_semantics=("arbitrary",), vmem_limit_bytes=VMEM_LIMIT),
        name="sample_attention",
    )(q, cache_k, cache_v, k_new, v_new, bias_c, bias_n, lam_vecs, subln_row)


def _ffn_kernel(x_ref, a_ref, hist_ref, nm_ref, wug_ref, bg_ref, wp_ref, ps_ref, wo_ref,
                nf_ref, wgu_ref, wd_ref, nfin_ref, y_ref, pst_ref, carry_ref, ext_ref,
                *, streams, rows, pos0, attn_transposed):
    t = pl.program_id(1)
    n = streams * rows
    d = D_MODEL

    @pl.when(t == 0)
    def _():
        carry_ref[...] = hist_ref[...]

    x = x_ref[...].reshape(n, d)
    h = _rmsnorm(x, nm_ref[...], EPS).astype(BF16)
    u = jnp.dot(h, wug_ref[:, 0:d], preferred_element_type=F32)
    zg = jnp.dot(h, wug_ref[:, d:3 * d], preferred_element_type=F32) + bg_ref[...]

    ext_ref[:, 0:HALO, :] = carry_ref[...]
    ext_ref[:, HALO:HALO + rows, :] = u.reshape(streams, rows, d)
    last = ext_ref[:, rows:rows + HALO, :]
    carry_ref[...] = last
    pst_ref[...] = last

    pos = pos0 + t * rows + lax.broadcasted_iota(jnp.int32, (1, rows, 1), 1)
    pooled = []
    for gi, w in enumerate(POOL_WINDOWS):
        cs = slice(gi * POOL_GROUP, (gi + 1) * POOL_GROUP)
        cur = ext_ref[:, HALO:HALO + rows, cs]
        s = cur
        for back in range(1, w):
            s = s + ext_ref[:, HALO - back:HALO - back + rows, cs]
        inv = 1.0 / jnp.minimum(pos + 1, w).astype(F32)
        dlt = (s * inv - cur).reshape(n, POOL_GROUP).astype(BF16)
        pooled.append(jnp.dot(dlt, wp_ref[gi], preferred_element_type=F32))
    pool = jnp.concatenate(pooled, axis=-1) * ps_ref[...]

    g_a = jax.nn.sigmoid(zg[:, 0:d])
    g_p = jax.nn.sigmoid(zg[:, d:2 * d])
    attn = a_ref[0].T if attn_transposed else a_ref[...].reshape(n, d)
    merged = g_a * attn + g_p * pool
    x1 = x + jnp.dot(merged.astype(BF16), wo_ref[...], preferred_element_type=F32)

    h2 = _rmsnorm(x1, nf_ref[...], EPS).astype(BF16)
    gu = jnp.dot(h2, wgu_ref[...], preferred_element_type=F32)
    act = jax.nn.silu(gu[:, 0:D_FF]) * gu[:, D_FF:2 * D_FF]
    x2 = x1 + jnp.dot(act.astype(BF16), wd_ref[...], preferred_element_type=F32)
    y_ref[...] = _rmsnorm(x2, nfin_ref[...], EPS).reshape(streams, rows, d)


def _merge_ffn(x, attn, hist, weights, *, streams, rows, pos0, attn_transposed):
    b, t, d = x.shape
    tok_spec = pl.BlockSpec((streams, rows, d), lambda i, j: (i, j, 0))
    halo_spec = pl.BlockSpec((streams, HALO, d), lambda i, j: (i, 0, 0))
    if attn_transposed:
        assert streams == 1
        attn_spec = pl.BlockSpec((1, d, rows), lambda i, j: (i, 0, j))
    else:
        attn_spec = tok_spec
    w_specs = [_const_spec(w.shape) for w in weights]
    w_specs[1] = _const_spec((d, 3 * d), index=(0, 1))
    return pl.pallas_call(
        functools.partial(_ffn_kernel, streams=streams, rows=rows, pos0=pos0,
                          attn_transposed=attn_transposed),
        grid=(b // streams, t // rows),
        in_specs=[tok_spec, attn_spec, halo_spec] + w_specs,
        out_specs=[tok_spec, halo_spec],
        out_shape=[jax.ShapeDtypeStruct((b, t, d), F32),
                   jax.ShapeDtypeStruct((b, HALO, d), F32)],
        scratch_shapes=[pltpu.VMEM((streams, HALO, d), F32),
                        pltpu.VMEM((streams, HALO + rows, d), F32)],
        compiler_params=pltpu.CompilerParams(
            dimension_semantics=("arbitrary", "arbitrary"), vmem_limit_bytes=VMEM_LIMIT),
        name="merge_ffn",
    )(x, attn, hist, *weights)


def kernel(x_prompt, x_sample, cache_k, cache_v, state_pool, rel_bias, norm_mix, w_in, b_gate,
           lambda_q1, lambda_k1, lambda_q2, lambda_k2, subln_g, w_pool, pool_scale, w_out,
           norm_ffn, w_gate_up, w_down, norm_final):
    assert norm_mix.shape[0] == 1, "single layer"
    d = D_MODEL
    bp, tp, _ = x_prompt.shape
    bs, ts, _ = x_sample.shape
    past = cache_k.shape[2]
    assert tp % PROJ_ROWS == 0 and tp % FFN_ROWS == 0 and PROJ_ROWS % ATTN_BLOCK == 0
    assert ATTN_BLOCK % CHUNK == 0 and N_HEADS % ATTN_HEADS == 0 and ts >= HALO and ts % SUBLANES == 0
    assert (past + ts - 1) // CHUNK <= past // CHUNK

    w_in_b = w_in.reshape(d, 6 * d).astype(BF16)
    ffn_weights = (norm_mix, w_in_b, b_gate, w_pool[0].astype(BF16), pool_scale,
                   w_out[0].astype(BF16), norm_ffn, w_gate_up[0].astype(BF16),
                   w_down[0].astype(BF16), norm_final.reshape(1, d))
    lam_vecs = jnp.concatenate([lambda_q1, lambda_k1, lambda_q2, lambda_k2], axis=0)

    k_p, v_p, kb, qt, vt = _project(x_prompt, norm_mix, w_in_b, rows=PROJ_ROWS, blocked=True)
    attn_p = _prompt_attention(qt, kb, vt, *_prompt_bias_tiles(rel_bias), lam_vecs,
                               subln_g.reshape(DV, 1))
    y_p, pool_p = _merge_ffn(x_prompt, attn_p, jnp.zeros((bp, HALO, d), F32), ffn_weights,
                             streams=1, rows=FFN_ROWS, pos0=0, attn_transposed=True)

    xs_flat = x_sample.reshape(1, bs * ts, d)
    q_s, k_s, v_s = _project(xs_flat, norm_mix, w_in_b, rows=bs * ts, blocked=False)
    q_s, k_s, v_s = (a.reshape(bs, ts, d) for a in (q_s, k_s, v_s))
    bias_c, bias_n = _sample_bias(rel_bias, past, ts)
    attn_s = _sample_attention(q_s, cache_k.reshape(bs, past * N_HEADS, DV),
                               cache_v.reshape(bs, past * N_HEADS, DV), k_s, v_s, bias_c, bias_n,
                               lam_vecs, subln_g.reshape(1, DV))
    hist_s = jnp.pad(state_pool[0], ((0, 0), (HALO - POOL_HIST, 0), (0, 0)))
    y_s, pool_s = _merge_ffn(x_sample, attn_s, hist_s, ffn_weights, streams=bs, rows=ts, pos0=past,
                             attn_transposed=False)

    heads = (N_HEADS, DV)
    return (y_p, y_s,
            k_p.reshape(1, bp, tp, *heads), v_p.reshape(1, bp, tp, *heads),
            pool_p[None, :, HALO - POOL_HIST:, :],
            k_s.reshape(1, bs, ts, *heads), v_s.reshape(1, bs, ts, *heads),
            pool_s[None, :, HALO - POOL_HIST:, :])
```

```python
import functools
import math

import numpy as np
import jax
import jax.numpy as jnp
from jax import lax
from jax.experimental import pallas as pl
from jax.experimental.pallas import tpu as pltpu

F32 = jnp.float32
BF16 = jnp.bfloat16

D_MODEL = 1024
CHUNK = 64
HEAD_DIM = 64
DV = 2 * HEAD_DIM
N_HEADS = D_MODEL // DV
POOL_WINDOWS = (2, 4, 8, 16)
POOL_GROUP = D_MODEL // len(POOL_WINDOWS)
POOL_HIST = max(POOL_WINDOWS) - 1
HALO = POOL_HIST + 1
D_FF = ((8 * D_MODEL) // 3 + 255) // 256 * 256
NUM_BUCKETS = 32
MAX_DISTANCE = 128
EPS = 1e-6
SUBLN_EPS = 1e-5
LAM_INIT = 0.8 - 0.6 * math.exp(-0.3 * 0)
LOG2E = math.log2(math.e)
QK_SCALE_LOG2 = HEAD_DIM ** -0.5 * LOG2E
MASKED = -1e30
SUM_LIMIT = 2.0 ** 16

ATTN_BLOCK = 512
ATTN_HEADS = 4
PROJ_ROWS = 1024
FFN_ROWS = 512
FAR_GROUP = 4
BIAS_CORNER = 128
SAMPLE_NEAR = 128
VMEM_LIMIT = 56 * 1024 * 1024
SUBLANES = 8


def _rmsnorm(x, g, eps):
    return x * lax.rsqrt(jnp.mean(x * x, axis=-1, keepdims=True) + eps) * g


def _const_spec(shape, index=None):
    index = (0,) * len(shape) if index is None else index
    return pl.BlockSpec(shape, lambda *_: index, pipeline_mode=pl.Buffered(1))


def _proj_kernel(x_ref, g_ref, w_ref, *out_refs, rows, blocked):
    x = x_ref[0]
    h = _rmsnorm(x, g_ref[...], EPS).astype(BF16)
    q = jnp.dot(h, w_ref[:, 0:D_MODEL], preferred_element_type=F32) * QK_SCALE_LOG2
    k = jnp.dot(h, w_ref[:, D_MODEL:2 * D_MODEL], preferred_element_type=F32)
    v = jnp.dot(h, w_ref[:, 2 * D_MODEL:3 * D_MODEL], preferred_element_type=F32)
    if not blocked:
        q_ref, k_ref, v_ref = out_refs
        q_ref[0] = q
        k_ref[0] = k
        v_ref[0] = v
        return
    k_ref, v_ref, kb_ref, qt_ref, vt_ref = out_refs
    qt = q.T
    vt = v.T
    for hd in range(N_HEADS):
        hs = slice(hd * DV, (hd + 1) * DV)
        k_ref[0, pl.ds(hd, rows, stride=N_HEADS), :] = k[:, hs]
        v_ref[0, pl.ds(hd, rows, stride=N_HEADS), :] = v[:, hs]
        for c in range(rows // ATTN_BLOCK):
            cs = slice(c * ATTN_BLOCK, (c + 1) * ATTN_BLOCK)
            kb_ref[0, hd, c] = k[cs, hs].astype(BF16)
            qt_ref[0, hd, c] = qt[hs, cs].astype(BF16)
            vt_ref[0, hd, c] = vt[hs, cs].astype(BF16)


def _project(x, norm_mix, w_in_b, *, rows, blocked):
    b, t, d = x.shape
    nb_step = rows // ATTN_BLOCK
    nat = jax.ShapeDtypeStruct((b, t, d), F32)
    nat_spec = pl.BlockSpec((1, rows, d), lambda i, j: (i, j, 0))
    if blocked:
        nb = t // ATTN_BLOCK
        per_head = jax.ShapeDtypeStruct((b, t * N_HEADS, DV), F32)
        per_head_spec = pl.BlockSpec((1, rows * N_HEADS, DV), lambda i, j: (i, j, 0))
        out_shape = [per_head, per_head,
                     jax.ShapeDtypeStruct((b, N_HEADS, nb, ATTN_BLOCK, DV), BF16),
                     jax.ShapeDtypeStruct((b, N_HEADS, nb, DV, ATTN_BLOCK), BF16),
                     jax.ShapeDtypeStruct((b, N_HEADS, nb, DV, ATTN_BLOCK), BF16)]
        out_specs = [per_head_spec, per_head_spec,
                     pl.BlockSpec((1, N_HEADS, nb_step, ATTN_BLOCK, DV), lambda i, j: (i, 0, j, 0, 0)),
                     pl.BlockSpec((1, N_HEADS, nb_step, DV, ATTN_BLOCK), lambda i, j: (i, 0, j, 0, 0)),
                     pl.BlockSpec((1, N_HEADS, nb_step, DV, ATTN_BLOCK), lambda i, j: (i, 0, j, 0, 0))]
    else:
        out_shape = [nat, nat, nat]
        out_specs = [nat_spec, nat_spec, nat_spec]
    return pl.pallas_call(
        functools.partial(_proj_kernel, rows=rows, blocked=blocked),
        grid=(b, t // rows),
        in_specs=[nat_spec, _const_spec((1, d)), _const_spec((d, 3 * d))],
        out_specs=out_specs,
        out_shape=out_shape,
        compiler_params=pltpu.CompilerParams(
            dimension_semantics=("arbitrary", "arbitrary"), vmem_limit_bytes=VMEM_LIMIT),
        name="qkv_proj_blocked" if blocked else "qkv_proj",
    )(x, norm_mix, w_in_b)


def _rel_bucket_np(rel):
    nb = NUM_BUCKETS // 2
    ret = np.where(rel > 0, nb, 0)
    n = np.abs(rel)
    max_exact = nb // 2
    large = max_exact + (np.log(np.maximum(n, 1).astype(np.float64) / max_exact)
                         / math.log(MAX_DISTANCE / max_exact) * (nb - max_exact)).astype(np.int64)
    large = np.minimum(large, nb - 1)
    return (ret + np.where(n < max_exact, n, large)).astype(np.int32)


FAR_BUCKET = int(_rel_bucket_np(np.array([-MAX_DISTANCE]))[0])
FAR_DISTANCE = int(np.min(np.nonzero(_rel_bucket_np(-np.arange(4 * MAX_DISTANCE)) == FAR_BUCKET)[0]))
assert np.all(_rel_bucket_np(-np.arange(FAR_DISTANCE, 1 << 16)) == FAR_BUCKET)


def _toeplitz(vec, rows, cols):
    h, length = vec.shape
    assert length == rows + cols - 1
    ext = jnp.concatenate([vec, jnp.zeros((h, 1), vec.dtype)], axis=1)
    skew = jnp.tile(ext, (1, rows))[:, :rows * length].reshape(h, rows, length)
    return skew[:, :, rows - 1:rows - 1 + cols]


def _bias_rows(rel_bias, rels):
    table = (rel_bias - rel_bias[FAR_BUCKET][None, :]).T * LOG2E
    return jnp.take(table, jnp.asarray(_rel_bucket_np(np.asarray(rels))), axis=1)


def _lambda_from(lv):
    s1 = jnp.sum(lv[0:1] * lv[1:2], axis=-1, keepdims=True)
    s2 = jnp.sum(lv[2:3] * lv[3:4], axis=-1, keepdims=True)
    return jnp.exp(s1) - jnp.exp(s2) + LAM_INIT


def _flash_kernel(qt_ref, kb_ref, vt_ref, bdiag_ref, bcorner_ref, lv_ref, g_ref, o_ref,
                  w_ref, m_ref, acc_ref, l_ref, s_ref, bias_ref, *, blk, heads):
    i = pl.program_id(2)
    half = blk // 2
    cn = BIAS_CORNER

    @pl.when(i == 0)
    def _():
        for hd in range(heads):
            corner = bcorner_ref[hd]
            diag = bdiag_ref[hd]
            bias_ref[hd, 0] = jnp.zeros((blk, blk), F32)
            bias_ref[hd, 0, blk - cn:blk, 0:cn] = corner
            bias_ref[hd, 1, 0:half, 0:half] = diag
            bias_ref[hd, 1, half:blk, half:blk] = diag
            bias_ref[hd, 1, 0:half, half:blk] = jnp.zeros((half, half), F32)
            bias_ref[hd, 1, half - cn:half, half:half + cn] = corner
            bias_ref[hd, 1, half:blk, 0:half] = jnp.full((half, half), MASKED, F32)

    zeros = jnp.zeros((HEAD_DIM, blk), BF16)
    for hd in range(heads):
        qt = qt_ref[0, hd, 0]
        w_ref[hd, 0] = jnp.concatenate([qt[:HEAD_DIM], zeros], axis=0)
        w_ref[hd, 1] = jnp.concatenate([zeros, qt[HEAD_DIM:]], axis=0)

    chains = [(hd, mp) for hd in range(heads) for mp in range(2)]

    def key_sums(p):
        return jnp.sum(p.astype(F32).reshape(p.shape[0] // SUBLANES, SUBLANES, p.shape[1]), axis=0)

    def diagonal_step(lookahead):
        def diag_scores(hd, mp):
            sa = jnp.dot(kb_ref[0, hd, i, 0:half, :], w_ref[hd, mp],
                         preferred_element_type=F32) + bias_ref[hd, 1, 0:half, :]
            sb = jnp.dot(kb_ref[0, hd, i, half:blk, :], w_ref[hd, mp, :, half:blk],
                         preferred_element_type=F32) + bias_ref[hd, 1, half:blk, half:blk]
            return sa, sb

        def diag_consume(hd, mp, sa, sb):
            m_r = jnp.maximum(jnp.max(sa[:, half:blk], axis=0, keepdims=True),
                              jnp.max(sb, axis=0, keepdims=True))
            m = jnp.concatenate([jnp.max(sa[:, 0:half], axis=0, keepdims=True), m_r], axis=1)
            m_ref[hd, mp] = m
            pa = jnp.exp2(sa - m)
            pb = jnp.exp2(sb - m_r)
            acc_ref[hd, mp] = jnp.dot(vt_ref[0, hd, i, :, 0:half], pa.astype(BF16),
                                      preferred_element_type=F32)
            acc_ref[hd, mp, :, half:blk] += jnp.dot(vt_ref[0, hd, i, :, half:blk], pb.astype(BF16),
                                                    preferred_element_type=F32)
            l_ref[hd, mp] = key_sums(pa)
            l_ref[hd, mp, :, half:blk] += key_sums(pb)

        pending = diag_scores(*chains[0])
        out = None
        for n, (hd, mp) in enumerate(chains):
            if n + 1 < len(chains):
                nxt = diag_scores(*chains[n + 1])
            else:
                nxt = None
                out = lookahead()
            diag_consume(hd, mp, *pending)
            pending = nxt
        return out

    def scores(j, tile, hd, mp):
        s = jnp.dot(kb_ref[0, hd, j], w_ref[hd, mp], preferred_element_type=F32)
        if tile is not None:
            s = s + bias_ref[hd, tile]
        return s

    def consume(j, hd, mp, s):
        p = jnp.exp2((s - m_ref[hd, mp]).astype(BF16))
        acc_ref[hd, mp] += jnp.dot(vt_ref[0, hd, j], p, preferred_element_type=F32)
        l_ref[hd, mp] += key_sums(p)

    def fast_step(j, tile, pending=None, lookahead=None):
        if pending is None:
            pending = scores(j, tile, *chains[0])
        out = None
        for n, (hd, mp) in enumerate(chains):
            if n + 1 < len(chains):
                nxt = scores(j, tile, *chains[n + 1])
            else:
                nxt = None
                out = lookahead() if lookahead is not None else None
            consume(j, hd, mp, pending)
            pending = nxt
        return out

    n_far = jnp.maximum(i - 1, 0)

    def far_scores(j):
        return scores(j, None, *chains[0])

    s_ref[...] = diagonal_step(lambda: scores(jnp.maximum(i - 1, 0), 0, *chains[0]))

    @pl.when(i >= 1)
    def _():
        s_ref[...] = fast_step(i - 1, 0, pending=s_ref[...], lookahead=lambda: far_scores(0))

    def far_group(j0, count):
        pending = s_ref[...]
        for c in range(count):
            nxt_block = jnp.minimum(j0 + c + 1, n_far - 1)
            pending = fast_step(j0 + c, None, pending=pending,
                                lookahead=lambda nb=nxt_block: far_scores(nb))
        s_ref[...] = pending

    @pl.loop(0, n_far // FAR_GROUP)
    def _(jj):
        far_group(FAR_GROUP * jj, FAR_GROUP)

    done = (n_far // FAR_GROUP) * FAR_GROUP
    size = FAR_GROUP // 2
    while size >= 1:
        @pl.when(((n_far - done) & size) != 0)
        def _(done=done, size=size):
            far_group(done, size)
        done = done + ((n_far - done) & size)
        size //= 2

    def safe_recompute():
        @pl.loop(0, 2 * heads)
        def _(c):
            hd = c // 2
            mp = c % 2
            m_ref[hd, mp] = jnp.full((1, blk), MASKED, F32)
            acc_ref[hd, mp] = jnp.zeros((DV, blk), F32)
            l_ref[hd, mp] = jnp.zeros((SUBLANES, blk), F32)

            @pl.loop(0, i + 1)
            def _(j):
                s = jnp.dot(kb_ref[0, hd, j], w_ref[hd, mp], preferred_element_type=F32)
                tile = jnp.clip(j - (i - 1), 0, 1)
                s = s + jnp.where(j >= i - 1, bias_ref[hd, tile], 0.0)
                m_old = m_ref[hd, mp]
                m_new = jnp.maximum(m_old, jnp.max(s, axis=0, keepdims=True))
                alpha = jnp.exp2(m_old - m_new)
                p = jnp.exp2((s - m_new).astype(BF16))
                m_ref[hd, mp] = m_new
                pv = jnp.dot(vt_ref[0, hd, j], p, preferred_element_type=F32)
                acc_ref[hd, mp] = alpha * acc_ref[hd, mp] + pv
                l_ref[hd, mp] = alpha * l_ref[hd, mp] + key_sums(p)

    lam = _lambda_from(lv_ref[...])

    def finalize():
        poison = jnp.zeros((1, 1), F32)
        for hd in range(heads):
            l1 = jnp.sum(l_ref[hd, 0], axis=0, keepdims=True)
            l2 = jnp.sum(l_ref[hd, 1], axis=0, keepdims=True)
            o = acc_ref[hd, 0] * (1.0 / l1) - lam * (acc_ref[hd, 1] * (1.0 / l2))
            ms = jnp.mean(o * o, axis=0, keepdims=True)
            y = o * (lax.rsqrt(ms + SUBLN_EPS) * (1.0 - LAM_INIT)) * g_ref[...]
            o_ref[0, hd * DV:(hd + 1) * DV, :] = y
            flags = (ms + l1 + l2) * 0.0 + jnp.where(jnp.maximum(l1, l2) > SUM_LIMIT, 1.0, 0.0)
            poison = poison + jnp.sum(flags, axis=1, keepdims=True)
        return poison

    poison = finalize()

    @pl.when(jnp.logical_not(poison[0, 0] == 0.0))
    def _():
        safe_recompute()
        finalize()


def _prompt_bias_tiles(rel_bias):
    half = ATTN_BLOCK // 2
    cn = BIAS_CORNER
    assert cn >= FAR_DISTANCE - 1 and half >= cn and half % CHUNK == 0
    j = np.arange(2 * half - 1)
    diag = _toeplitz(_bias_rows(rel_bias, half - 1 - j), half, half)
    kk = np.arange(half)[:, None]
    qq = np.arange(half)[None, :]
    visible = jnp.asarray((kk // CHUNK) <= (qq // CHUNK))
    j = np.arange(2 * cn - 1)
    corner = _toeplitz(_bias_rows(rel_bias, -1 - j), cn, cn)
    return jnp.where(visible[None], diag, MASKED), corner


def _prompt_attention(qt, kb, vt, bias_diag, bias_corner, lam_vecs, subln_col):
    b, nh, nb, _, blk = qt.shape
    t = nb * blk
    hps = ATTN_HEADS
    return pl.pallas_call(
        functools.partial(_flash_kernel, blk=blk, heads=hps),
        grid=(b, nh // hps, nb),
        in_specs=[
            pl.BlockSpec((1, hps, 1, DV, blk), lambda bi, h, i: (bi, h, i, 0, 0)),
            pl.BlockSpec((1, hps, nb, blk, DV), lambda bi, h, i: (bi, h, 0, 0, 0)),
            pl.BlockSpec((1, hps, nb, DV, blk), lambda bi, h, i: (bi, h, 0, 0, 0)),
            pl.BlockSpec((hps, blk // 2, blk // 2), lambda bi, h, i: (h, 0, 0),
                         pipeline_mode=pl.Buffered(1)),
            pl.BlockSpec((hps, BIAS_CORNER, BIAS_CORNER), lambda bi, h, i: (h, 0, 0),
                         pipeline_mode=pl.Buffered(1)),
            _const_spec((4, HEAD_DIM)),
            _const_spec((DV, 1)),
        ],
        out_specs=pl.BlockSpec((1, hps * DV, blk), lambda bi, h, i: (bi, h, i)),
        out_shape=jax.ShapeDtypeStruct((b, nh * DV, t), F32),
        scratch_shapes=[pltpu.VMEM((hps, 2, DV, blk), BF16),
                        pltpu.VMEM((hps, 2, 1, blk), F32),
                        pltpu.VMEM((hps, 2, DV, blk), F32),
                        pltpu.VMEM((hps, 2, SUBLANES, blk), F32),
                        pltpu.VMEM((blk, blk), F32),
                        pltpu.VMEM((hps, 2, blk, blk), F32)],
        compiler_params=pltpu.CompilerParams(
            dimension_semantics=("arbitrary", "arbitrary", "arbitrary"),
            vmem_limit_bytes=VMEM_LIMIT),
        name="prompt_attention",
    )(qt, kb, vt, bias_diag, bias_corner, lam_vecs, subln_col)


def _sample_attn_kernel(q_ref, ck_ref, cv_ref, kn_ref, vn_ref, bc_ref, bn_ref, lv_ref, g_ref, o_ref,
                        *, past, far):
    lam = _lambda_from(lv_ref[...])
    q = q_ref[0]
    kn = kn_ref[0]
    vn = vn_ref[0]
    s_new = q.shape[0]
    lane = lax.broadcasted_iota(jnp.int32, (s_new, DV), 1)
    nt = (((1,), (1,)), ((), ()))

    def head_scores(hd):
        hs = slice(hd * DV, (hd + 1) * DV)
        qh = q[:, hs]
        qq = jnp.concatenate([jnp.where(lane < HEAD_DIM, qh, 0.0),
                              jnp.where(lane >= HEAD_DIM, qh, 0.0)], axis=0).astype(BF16)
        kc = ck_ref[0, pl.ds(hd, past, stride=N_HEADS), :].astype(BF16)
        sf = lax.dot_general(qq, kc[:far], nt, preferred_element_type=F32)
        sc = lax.dot_general(qq, kc[far:], nt, preferred_element_type=F32) + bc_ref[hd]
        sn = lax.dot_general(qq, kn[:, hs].astype(BF16), nt, preferred_element_type=F32) + bn_ref[hd]
        return sf, sc, sn

    def head_output(hd, sf, sc, sn):
        hs = slice(hd * DV, (hd + 1) * DV)
        vc = cv_ref[0, pl.ds(hd, past, stride=N_HEADS), :].astype(BF16)
        m = jnp.maximum(jnp.maximum(jnp.max(sf, axis=-1, keepdims=True),
                                    jnp.max(sc, axis=-1, keepdims=True)),
                        jnp.max(sn, axis=-1, keepdims=True))
        pf = jnp.exp2(sf - m)
        pc = jnp.exp2(sc - m)
        pn = jnp.exp2(sn - m)
        l = (jnp.sum(pf, axis=-1, keepdims=True) + jnp.sum(pc, axis=-1, keepdims=True)
             + jnp.sum(pn, axis=-1, keepdims=True))
        o = (jnp.dot(pf.astype(BF16), vc[:far], preferred_element_type=F32)
             + jnp.dot(pc.astype(BF16), vc[far:], preferred_element_type=F32)
             + jnp.dot(pn.astype(BF16), vn[:, hs].astype(BF16), preferred_element_type=F32))
        o = o * (1.0 / l)
        o = o[0:s_new] - lam * o[s_new:2 * s_new]
        o_ref[0, :, hs] = _rmsnorm(o, g_ref[...], SUBLN_EPS) * (1.0 - LAM_INIT)

    pending = head_scores(0)
    for hd in range(N_HEADS):
        nxt = head_scores(hd + 1) if hd + 1 < N_HEADS else None
        head_output(hd, *pending)
        pending = nxt


def _sample_bias(rel_bias, past, s_new):
    near = SAMPLE_NEAR
    j = np.arange(s_new + near - 1)
    bias_c = _toeplitz(_bias_rows(rel_bias, j - (s_new - 1) - near), s_new, near)
    j = np.arange(2 * s_new - 1)
    bias_n = _toeplitz(_bias_rows(rel_bias, j - (s_new - 1)), s_new, s_new)
    twice = lambda t: jnp.concatenate([t, t], axis=1)
    return twice(bias_c), twice(bias_n)


def _sample_attention(q, cache_k, cache_v, k_new, v_new, bias_c, bias_n, lam_vecs, subln_row):
    b, s, d = q.shape
    past = cache_k.shape[1] // N_HEADS
    far = past - SAMPLE_NEAR
    assert far >= 0 and SAMPLE_NEAR >= FAR_DISTANCE
    new_spec = pl.BlockSpec((1, s, d), lambda i: (i, 0, 0))
    cache_spec = pl.BlockSpec((1, past * N_HEADS, DV), lambda i: (i, 0, 0))
    return pl.pallas_call(
        functools.partial(_sample_attn_kernel, past=past, far=far),
        grid=(b,),
        in_specs=[new_spec, cache_spec, cache_spec, new_spec, new_spec,
                  _const_spec(bias_c.shape), _const_spec(bias_n.shape),
                  _const_spec((4, HEAD_DIM)), _const_spec((1, DV))],
        out_specs=new_spec,
        out_shape=jax.ShapeDtypeStruct((b, s, d), F32),
        compiler_params=pltpu.CompilerParams(
            dimension_semantics=("arbitrary",), vmem_limit_bytes=VMEM_LIMIT),
        name="sample_attention",
    )(q, cache_k, cache_v, k_new, v_new, bias_c, bias_n, lam_vecs, subln_row)


def _ffn_kernel(x_ref, a_ref, hist_ref, nm_ref, wug_ref, bg_ref, wp_ref, ps_ref, wo_ref,
                nf_ref, wgu_ref, wd_ref, nfin_ref, y_ref, pst_ref, carry_ref, ext_ref,
                *, streams, rows, pos0, attn_transposed):
    t = pl.program_id(1)
    n = streams * rows
    d = D_MODEL

    @pl.when(t == 0)
    def _():
        carry_ref[...] = hist_ref[...]

    x = x_ref[...].reshape(n, d)
    r1 = lax.rsqrt(jnp.mean(x * x, axis=-1, keepdims=True) + EPS)
    h = (x * nm_ref[...]).astype(BF16)
    u = jnp.dot(h, wug_ref[:, 0:d], preferred_element_type=F32) * r1
    zg = jnp.dot(h, wug_ref[:, d:3 * d], preferred_element_type=F32) * r1 + bg_ref[...]

    ext_ref[:, 0:HALO, :] = carry_ref[...]
    ext_ref[:, HALO:HALO + rows, :] = u.reshape(streams, rows, d)
    last = ext_ref[:, rows:rows + HALO, :]
    carry_ref[...] = last
    pst_ref[...] = last

    pos = pos0 + t * rows + lax.broadcasted_iota(jnp.int32, (1, rows, 1), 1)
    pooled = []
    for gi, w in enumerate(POOL_WINDOWS):
        cs = slice(gi * POOL_GROUP, (gi + 1) * POOL_GROUP)
        cur = ext_ref[:, HALO:HALO + rows, cs]
        s = cur
        for back in range(1, w):
            s = s + ext_ref[:, HALO - back:HALO - back + rows, cs]
        inv = 1.0 / jnp.minimum(pos + 1, w).astype(F32)
        dlt = (s * inv - cur).reshape(n, POOL_GROUP).astype(BF16)
        pooled.append(jnp.dot(dlt, wp_ref[gi], preferred_element_type=F32))
    pool = jnp.concatenate(pooled, axis=-1) * ps_ref[...]

    g_a = jax.nn.sigmoid(zg[:, 0:d])
    g_p = jax.nn.sigmoid(zg[:, d:2 * d])
    attn = a_ref[0].T if attn_transposed else a_ref[...].reshape(n, d)
    merged = g_a * attn + g_p * pool
    x1 = x + jnp.dot(merged.astype(BF16), wo_ref[...], preferred_element_type=F32)

    r2 = lax.rsqrt(jnp.mean(x1 * x1, axis=-1, keepdims=True) + EPS)
    gu = jnp.dot((x1 * nf_ref[...]).astype(BF16), wgu_ref[...], preferred_element_type=F32)
    act = jax.nn.silu(gu[:, 0:D_FF] * r2) * (gu[:, D_FF:2 * D_FF] * r2)
    x2 = x1 + jnp.dot(act.astype(BF16), wd_ref[...], preferred_element_type=F32)
    y_ref[...] = _rmsnorm(x2, nfin_ref[...], EPS).reshape(streams, rows, d)


def _merge_ffn(x, attn, hist, weights, *, streams, rows, pos0, attn_transposed):
    b, t, d = x.shape
    tok_spec = pl.BlockSpec((streams, rows, d), lambda i, j: (i, j, 0))
    halo_spec = pl.BlockSpec((streams, HALO, d), lambda i, j: (i, 0, 0))
    if attn_transposed:
        assert streams == 1
        attn_spec = pl.BlockSpec((1, d, rows), lambda i, j: (i, 0, j))
    else:
        attn_spec = tok_spec
    w_specs = [_const_spec(w.shape) for w in weights]
    w_specs[1] = _const_spec((d, 3 * d), index=(0, 1))
    return pl.pallas_call(
        functools.partial(_ffn_kernel, streams=streams, rows=rows, pos0=pos0,
                          attn_transposed=attn_transposed),
        grid=(b // streams, t // rows),
        in_specs=[tok_spec, attn_spec, halo_spec] + w_specs,
        out_specs=[tok_spec, halo_spec],
        out_shape=[jax.ShapeDtypeStruct((b, t, d), F32),
                   jax.ShapeDtypeStruct((b, HALO, d), F32)],
        scratch_shapes=[pltpu.VMEM((streams, HALO, d), F32),
                        pltpu.VMEM((streams, HALO + rows, d), F32)],
        compiler_params=pltpu.CompilerParams(
            dimension_semantics=("arbitrary", "arbitrary"), vmem_limit_bytes=VMEM_LIMIT),
        name="merge_ffn",
    )(x, attn, hist, *weights)


def kernel(x_prompt, x_sample, cache_k, cache_v, state_pool, rel_bias, norm_mix, w_in, b_gate,
           lambda_q1, lambda_k1, lambda_q2, lambda_k2, subln_g, w_pool, pool_scale, w_out,
           norm_ffn, w_gate_up, w_down, norm_final):
    assert norm_mix.shape[0] == 1, "single layer"
    d = D_MODEL
    bp, tp, _ = x_prompt.shape
    bs, ts, _ = x_sample.shape
    past = cache_k.shape[2]
    assert tp % PROJ_ROWS == 0 and tp % FFN_ROWS == 0 and PROJ_ROWS % ATTN_BLOCK == 0
    assert ATTN_BLOCK % CHUNK == 0 and N_HEADS % ATTN_HEADS == 0 and ts >= HALO and ts % SUBLANES == 0
    assert (past + ts - 1) // CHUNK <= past // CHUNK

    w_in_b = w_in.reshape(d, 6 * d).astype(BF16)
    ffn_weights = (norm_mix, w_in_b, b_gate, w_pool[0].astype(BF16), pool_scale,
                   w_out[0].astype(BF16), norm_ffn, w_gate_up[0].astype(BF16),
                   w_down[0].astype(BF16), norm_final.reshape(1, d))
    lam_vecs = jnp.concatenate([lambda_q1, lambda_k1, lambda_q2, lambda_k2], axis=0)

    k_p, v_p, kb, qt, vt = _project(x_prompt, norm_mix, w_in_b, rows=PROJ_ROWS, blocked=True)
    attn_p = _prompt_attention(qt, kb, vt, *_prompt_bias_tiles(rel_bias), lam_vecs,
                               subln_g.reshape(DV, 1))
    y_p, pool_p = _merge_ffn(x_prompt, attn_p, jnp.zeros((bp, HALO, d), F32), ffn_weights,
                             streams=1, rows=FFN_ROWS, pos0=0, attn_transposed=True)

    xs_flat = x_sample.reshape(1, bs * ts, d)
    q_s, k_s, v_s = _project(xs_flat, norm_mix, w_in_b, rows=bs * ts, blocked=False)
    q_s, k_s, v_s = (a.reshape(bs, ts, d) for a in (q_s, k_s, v_s))
    bias_c, bias_n = _sample_bias(rel_bias, past, ts)
    attn_s = _sample_attention(q_s, cache_k.reshape(bs, past * N_HEADS, DV),
                               cache_v.reshape(bs, past * N_HEADS, DV), k_s, v_s, bias_c, bias_n,
                               lam_vecs, subln_g.reshape(1, DV))
    hist_s = jnp.pad(state_pool[0], ((0, 0), (HALO - POOL_HIST, 0), (0, 0)))
    y_s, pool_s = _merge_ffn(x_sample, attn_s, hist_s, ffn_weights, streams=bs, rows=ts, pos0=past,
                             attn_transposed=False)

    heads = (N_HEADS, DV)
    return (y_p, y_s,
            k_p.reshape(1, bp, tp, *heads), v_p.reshape(1, bp, tp, *heads),
            pool_p[None, :, HALO - POOL_HIST:, :],
            k_s.reshape(1, bs, ts, *heads), v_s.reshape(1, bs, ts, *heads),
            pool_s[None, :, HALO - POOL_HIST:, :])
```
